```python
import jax, jax.numpy as jnp
from jax import lax
import numpy as np

D_MODEL = 1024
BATCH = 8
SEQ = 2048
DEPTH = 2
DEC_BATCH = 128
DEC_SEQ = 8
PAST_LEN = 16384
PAGE_SIZE = 128

HEAD_DIM = 64
D_A = D_MODEL // 2
N_HEADS_A = D_A // HEAD_DIM
LORA_W = 64
LORA_A = 64
LORA_G = 128
D_SHIFT = 3 * D_A + LORA_W + LORA_A + LORA_G
D_POOL = D_MODEL - D_A
POOL_WINDOWS = (2, 4, 8, 16)
N_POOL_GROUPS = len(POOL_WINDOWS)
POOL_GD = D_POOL // N_POOL_GROUPS
POOL_BUF = max(POOL_WINDOWS) - 1
D_IN = D_SHIFT + D_POOL + 2 * D_MODEL
D_FF = 4 * D_MODEL
RMS_EPS = 1e-6
GN_EPS = HEAD_DIM * 1e-5
L2_EPS = 1e-12

kernel_name = 'rwkv7_multiscale_pool_gated_hybrid_step'


def _rmsnorm(x, g):
    xf = x.astype(jnp.float32)
    y = xf * lax.rsqrt(jnp.mean(xf * xf, axis=-1, keepdims=True) + RMS_EPS)
    return (y * g.astype(jnp.float32)).astype(x.dtype)


def _wkv7_scan(r, w, k, v, a, b, s0):
    def step(s, inp):
        rt, wt, kt, vt, at, bt = inp
        sa = jnp.einsum('bhij,bhj->bhi', s, at)
        s = s * wt[:, :, None, :] + sa[..., None] * bt[:, :, None, :] + vt[..., None] * kt[:, :, None, :]
        y = jnp.einsum('bhij,bhj->bhi', s, rt)
        return s, y
    xs = (jnp.moveaxis(r, 1, 0), jnp.moveaxis(w, 1, 0), jnp.moveaxis(k, 1, 0),
          jnp.moveaxis(v, 1, 0), jnp.moveaxis(a, 1, 0), jnp.moveaxis(b, 1, 0))
    s_final, ys = lax.scan(step, s0, xs)
    return jnp.moveaxis(ys, 0, 1), s_final


def _mixer(h, shift_prev, pool_prev, wkv_prev, pos0,
           w_in, mu_shift, decay0, w_decay2, a0, w_a2, w_g2, k_k, k_a, r_k,
           ln_x_g, ln_x_b, w_a_up, w_pool, pool_scale, w_b_up, w_o):
    f32 = jnp.float32
    bsz, t_len, _ = h.shape
    z = h @ w_in
    z_rw = z[..., :D_SHIFT]
    u = z[..., D_SHIFT:D_SHIFT + D_POOL]
    gate_a = jax.nn.sigmoid(z[..., D_SHIFT + D_POOL:D_SHIFT + D_POOL + D_MODEL].astype(f32))
    gate_b = jax.nn.sigmoid(z[..., D_SHIFT + D_POOL + D_MODEL:].astype(f32))

    z_prev = jnp.concatenate([shift_prev[:, None, :].astype(z.dtype), z_rw[:, :-1]], axis=1)
    zs = (z_rw + (z_prev - z_rw) * mu_shift).astype(f32)
    o1, o2, o3 = D_A, 2 * D_A, 3 * D_A
    o4 = o3 + LORA_W
    o5 = o4 + LORA_A
    r = zs[..., :o1]
    k = zs[..., o1:o2]
    v = zs[..., o2:o3]
    lw = zs[..., o3:o4]
    la = zs[..., o4:o5]
    lg = zs[..., o5:]
    w_log = -jax.nn.softplus(-(decay0.astype(f32) + jnp.tanh(lw) @ w_decay2.astype(f32))) - 0.5
    decay = jnp.exp(-jnp.exp(w_log))
    a_in = jax.nn.sigmoid(a0.astype(f32) + la @ w_a2.astype(f32))
    g = jax.nn.sigmoid(lg) @ w_g2.astype(f32)
    hs = (bsz, t_len, N_HEADS_A, HEAD_DIM)
    kk = (k * k_k.astype(f32)).reshape(hs)
    kk = kk / jnp.maximum(jnp.sqrt(jnp.sum(kk * kk, axis=-1, keepdims=True)), L2_EPS)
    k = k * (1.0 + (a_in - 1.0) * k_a.astype(f32))
    rh = r.reshape(hs)
    kh = k.reshape(hs)
    vh = v.reshape(hs)
    ah = a_in.reshape(hs)
    y, wkv_new = _wkv7_scan(rh, decay.reshape(hs), kh, vh, -kk, kk * ah, wkv_prev.astype(f32))
    mu = jnp.mean(y, axis=-1, keepdims=True)
    var = jnp.mean(jnp.square(y - mu), axis=-1, keepdims=True)
    yn = ((y - mu) * lax.rsqrt(var + GN_EPS)).reshape(bsz, t_len, D_A)
    yn = yn * ln_x_g.astype(f32) + ln_x_b.astype(f32)
    bonus = (jnp.sum(rh * kh * r_k.astype(f32), axis=-1, keepdims=True) * vh).reshape(bsz, t_len, D_A)
    a_out = (((yn + bonus) * g).astype(h.dtype)) @ w_a_up

    full = jnp.concatenate([pool_prev.astype(u.dtype), u], axis=1)
    csum = lax.cumsum(full.astype(f32), axis=1)
    csum = jnp.concatenate([jnp.zeros_like(csum[:, :1]), csum], axis=1)
    pos = pos0 + jnp.arange(t_len)
    means = []
    for gi, win in enumerate(POOL_WINDOWS):
        sl = slice(gi * POOL_GD, (gi + 1) * POOL_GD)
        win_sum = (csum[:, POOL_BUF + 1:POOL_BUF + 1 + t_len, sl]
                   - csum[:, POOL_BUF + 1 - win:POOL_BUF + 1 - win + t_len, sl])
        cnt = jnp.minimum(pos + 1, win).astype(f32)[None, :, None]
        means.append(win_sum / cnt)
    p = (jnp.concatenate(means, axis=-1) - u.astype(f32)).astype(h.dtype)
    p = p.reshape(bsz, t_len, N_POOL_GROUPS, POOL_GD)
    p = jnp.einsum('btgc,gcd->btgd', p, w_pool).reshape(bsz, t_len, D_POOL) * pool_scale
    b_out = p @ w_b_up

    merged = (gate_a * a_out + gate_b * b_out).astype(h.dtype)
    out = merged @ w_o
    return out, z_rw[:, -1], full[:, -POOL_BUF:], wkv_new.astype(wkv_prev.dtype)


def _trunk(x, shift0, pool0, wkv0, pos0, norm1_g, norm2_g, final_norm_g, mixer_w, w_ff1, w_ff2):
    shifts, pools, wkvs = [], [], []
    for l in range(DEPTH):
        h = _rmsnorm(x, norm1_g[l])
        m, s_sh, s_pl, s_wk = _mixer(h, shift0[l], pool0[l], wkv0[l], pos0,
                                     *[wt[l] for wt in mixer_w])
        x = x + m
        h = _rmsnorm(x, norm2_g[l])
        x = x + jnp.square(jax.nn.relu(h @ w_ff1[l])) @ w_ff2[l]
        shifts.append(s_sh)
        pools.append(s_pl)
        wkvs.append(s_wk)
    return _rmsnorm(x, final_norm_g), jnp.stack(shifts), jnp.stack(pools), jnp.stack(wkvs)


def setup_inputs(seed: int = 0) -> dict:
    key = jax.random.key(seed)
    ks = jax.random.split(key, 32)

    def nrm(k, shape, scale):
        return jax.random.normal(k, shape, jnp.float32) * scale

    L = DEPTH
    return {
        'x_prompt': nrm(ks[0], (BATCH, SEQ, D_MODEL), 1.0),
        'x_sample': nrm(ks[1], (DEC_BATCH, DEC_SEQ, D_MODEL), 1.0),
        'state_shift': nrm(ks[2], (L, DEC_BATCH, D_SHIFT), 1.0),
        'state_pool': nrm(ks[3], (L, DEC_BATCH, POOL_BUF, D_POOL), 1.0),
        'state_wkv': nrm(ks[4], (L, DEC_BATCH, N_HEADS_A, HEAD_DIM, HEAD_DIM), 0.3),
        'norm1_g': 1.0 + nrm(ks[5], (L, D_MODEL), 0.05),
        'w_in': nrm(ks[6], (L, D_MODEL, D_IN), D_MODEL ** -0.5),
        'mu_shift': jax.random.uniform(ks[7], (L, D_SHIFT), jnp.float32, 0.1, 0.9),
        'decay0': -1.0 + nrm(ks[8], (L, D_A), 0.5),
        'w_decay2': nrm(ks[9], (L, LORA_W, D_A), 0.1 * LORA_W ** -0.5),
        'a0': nrm(ks[10], (L, D_A), 0.1),
        'w_a2': nrm(ks[11], (L, LORA_A, D_A), 0.5 * LORA_A ** -0.5),
        'w_g2': nrm(ks[12], (L, LORA_G, D_A), LORA_G ** -0.5),
        'k_k': 0.85 + nrm(ks[13], (L, D_A), 0.05),
        'k_a': 1.0 + nrm(ks[14], (L, D_A), 0.05),
        'r_k': nrm(ks[15], (L, N_HEADS_A, HEAD_DIM), 0.1),
        'ln_x_g': 1.0 + nrm(ks[16], (L, D_A), 0.05),
        'ln_x_b': nrm(ks[17], (L, D_A), 0.02),
        'w_a_up': nrm(ks[18], (L, D_A, D_MODEL), D_A ** -0.5),
        'w_pool': nrm(ks[19], (L, N_POOL_GROUPS, POOL_GD, POOL_GD), POOL_GD ** -0.5),
        'pool_scale': 1.0 + nrm(ks[20], (L, D_POOL), 0.1),
        'w_b_up': nrm(ks[21], (L, D_POOL, D_MODEL), D_POOL ** -0.5),
        'w_o': nrm(ks[22], (L, D_MODEL, D_MODEL), D_MODEL ** -0.5),
        'norm2_g': 1.0 + nrm(ks[23], (L, D_MODEL), 0.05),
        'w_ff1': nrm(ks[24], (L, D_MODEL, D_FF), D_MODEL ** -0.5),
        'w_ff2': nrm(ks[25], (L, D_FF, D_MODEL), D_FF ** -0.5),
        'final_norm_g': 1.0 + nrm(ks[26], (D_MODEL,), 0.05),
    }


def reference(x_prompt, x_sample, state_shift, state_pool, state_wkv,
              norm1_g, w_in, mu_shift, decay0, w_decay2, a0, w_a2, w_g2, k_k, k_a, r_k,
              ln_x_g, ln_x_b, w_a_up, w_pool, pool_scale, w_b_up, w_o,
              norm2_g, w_ff1, w_ff2, final_norm_g):
    mixer_w = (w_in, mu_shift, decay0, w_decay2, a0, w_a2, w_g2, k_k, k_a, r_k,
               ln_x_g, ln_x_b, w_a_up, w_pool, pool_scale, w_b_up, w_o)
    bsz = x_prompt.shape[0]
    shift_zero = jnp.zeros((DEPTH, bsz, D_SHIFT), state_shift.dtype)
    pool_zero = jnp.zeros((DEPTH, bsz, POOL_BUF, D_POOL), state_pool.dtype)
    wkv_zero = jnp.zeros((DEPTH, bsz, N_HEADS_A, HEAD_DIM, HEAD_DIM), state_wkv.dtype)
    y_prompt, p_shift, p_pool, p_wkv = _trunk(
        x_prompt, shift_zero, pool_zero, wkv_zero, 0,
        norm1_g, norm2_g, final_norm_g, mixer_w, w_ff1, w_ff2)
    y_sample, s_shift, s_pool, s_wkv = _trunk(
        x_sample, state_shift, state_pool, state_wkv, PAST_LEN,
        norm1_g, norm2_g, final_norm_g, mixer_w, w_ff1, w_ff2)
    return (y_prompt, y_sample, p_shift, p_pool, p_wkv, s_shift, s_pool, s_wkv)
```

```python
import functools

import jax
import jax.numpy as jnp
from jax import lax
from jax.experimental import pallas as pl
from jax.experimental.pallas import tpu as pltpu

F32 = jnp.float32
BF16 = jnp.bfloat16

D_MODEL = 1024
HEAD_DIM = 64
D_A = 512
N_HEADS = 8
LORA_WA = 128
LORA_G = 128
D_SHIFT = 3 * D_A + LORA_WA + LORA_G
D_POOL = 512
POOL_WINDOWS = (2, 4, 8, 16)
POOL_GD = 128
POOL_CARRY = 16
D_FF = 4096
RMS_EPS = 1e-6
GN_EPS = HEAD_DIM * 1e-5
L2_EPS = 1e-12

LANES = 128
SUBLANES = 8
CHUNK = 64
PAIRS = D_A // LANES
VMEM_LIMIT = 56 * 1024 * 1024


def _dot(a, b):
    return jnp.dot(a.astype(BF16), b.astype(BF16), preferred_element_type=F32)


def _dot_nt(a, b):
    return lax.dot_general(a.astype(BF16), b.astype(BF16), (((1,), (1,)), ((), ())),
                           preferred_element_type=F32)


def _dot_tn(a, b):
    return lax.dot_general(a.astype(BF16), b.astype(BF16), (((0,), (0,)), ((), ())),
                           preferred_element_type=F32)


def _split3(x):
    hi = x.astype(BF16)
    r1 = x - hi.astype(F32)
    mid = r1.astype(BF16)
    lo = (r1 - mid.astype(F32)).astype(BF16)
    return hi, mid, lo


def _dot_sel(sel, x):
    hi, mid, lo = _split3(x)
    d = lambda p: jnp.dot(sel, p, preferred_element_type=F32)
    return d(hi) + d(mid) + d(lo)


def _seg_sum(x, ones_bd):
    hi, mid, lo = _split3(x)
    d = lambda p: jnp.dot(p, ones_bd, preferred_element_type=F32)
    return d(hi) + d(mid) + d(lo)


def _div_pow2(x, d):
    assert d & (d - 1) == 0
    return jnp.right_shift(x, d.bit_length() - 1)


def _rms(x, g):
    return x * lax.rsqrt(jnp.mean(x * x, axis=-1, keepdims=True) + RMS_EPS) * g


def _sigmoid(x):
    return 1.0 / (1.0 + jnp.exp(-x))


def _softplus(x):
    return jnp.maximum(x, 0.0) + jnp.log(1.0 + jnp.exp(-jnp.abs(x)))


def _prep_kernel(x_ref, shift_ref, pool_ref, n1_ref, win_ref, mu_ref, dec0_ref, wdec_ref, a0_ref,
                 wa_ref, wg_ref, kk_ref, ka_ref, ones_ref, wpool_ref, pscale_ref, wbup_ref,
                 r_ref, lw_ref, k_ref, v_ref, a_ref, b_ref, g_ref, ga_ref, gbb_ref,
                 shift_out_ref, pool_out_ref, cz_ref, cu_ref, *, nb, tt, pos0):
    tm = tt * nb
    cz = cz_ref.shape[0]
    cu = cu_ref.shape[0]
    t = pl.program_id(1)

    @pl.when(t == 0)
    def _():
        cz_ref[...] = shift_ref[...].reshape(cz_ref.shape)
        cu_ref[...] = pool_ref[...].reshape(cu_ref.shape)

    x = x_ref[...].reshape(tm, D_MODEL)
    hb = _rms(x, n1_ref[...]).astype(BF16)

    z_rw = jnp.dot(hb, win_ref[:, 0:D_SHIFT], preferred_element_type=F32)
    ext = jnp.concatenate([cz_ref[...], z_rw], axis=0)
    z_prev = pltpu.roll(ext, nb, 0)[cz:]
    zs = z_rw + (z_prev - z_rw) * mu_ref[...]
    new_cz = z_rw[tm - cz:]
    cz_ref[...] = new_cz
    shift_out_ref[...] = new_cz.reshape(shift_out_ref.shape)

    r = zs[:, 0:D_A]
    k = zs[:, D_A:2 * D_A]
    v = zs[:, 2 * D_A:3 * D_A]
    lwa = zs[:, 3 * D_A:3 * D_A + LORA_WA]
    lg = zs[:, 3 * D_A + LORA_WA:D_SHIFT]
    w_log = -_softplus(-(dec0_ref[...] + _dot(jnp.tanh(lwa), wdec_ref[...]))) - 0.5
    a_in = _sigmoid(a0_ref[...] + _dot(lwa, wa_ref[...]))
    kk = k * kk_ref[...]
    kk = kk / jnp.maximum(jnp.sqrt(_seg_sum(kk * kk, ones_ref[...])), L2_EPS)
    osh = r_ref.shape
    r_ref[...] = r.reshape(osh)
    lw_ref[...] = (-jnp.exp(w_log)).reshape(osh)
    k_ref[...] = (k * (1.0 + (a_in - 1.0) * ka_ref[...])).reshape(osh)
    v_ref[...] = v.reshape(osh)
    a_ref[...] = (-kk).reshape(osh)
    b_ref[...] = (kk * a_in).reshape(osh)
    g_ref[...] = _dot(_sigmoid(lg), wg_ref[...]).reshape(osh)

    u = jnp.dot(hb, win_ref[:, D_SHIFT:D_SHIFT + D_POOL], preferred_element_type=F32)
    extu = jnp.concatenate([cu_ref[...], u], axis=0)
    new_cu = extu[tm:]
    cu_ref[...] = new_cu
    pool_out_ref[...] = new_cu.reshape(pool_out_ref.shape)
    row = lax.broadcasted_iota(jnp.int32, (tm, 1), 0)
    step = _div_pow2(row, nb)
    pos = pos0 + t * tt + step
    parts = []
    for gi, win in enumerate(POOL_WINDOWS):
        sl = slice(gi * POOL_GD, (gi + 1) * POOL_GD)
        s = extu[:, sl]
        span = 1
        while span < win:
            s = s + pltpu.roll(s, span * nb, 0)
            span *= 2
        cnt = jnp.minimum(pos + 1, win).astype(F32)
        p = s[cu:] / cnt - u[:, sl]
        parts.append(_dot(p, wpool_ref[gi]))
    pooled = jnp.concatenate(parts, axis=-1) * pscale_ref[...]
    b_out = _dot(pooled, wbup_ref[...])

    o2 = ga_ref.shape
    c0 = D_SHIFT + D_POOL
    ga_ref[...] = _sigmoid(jnp.dot(hb, win_ref[:, c0:c0 + D_MODEL],
                                   preferred_element_type=F32)).reshape(o2)
    gate_b = _sigmoid(jnp.dot(hb, win_ref[:, c0 + D_MODEL:c0 + 2 * D_MODEL],
                              preferred_element_type=F32))
    gbb_ref[...] = (gate_b * b_out).reshape(o2)


def _const_spec(shape):
    nd = len(shape)
    return pl.BlockSpec(shape, lambda *_: (0,) * nd, pipeline_mode=pl.Buffered(1))


def _prep_call(x, shift_in, pool_in, wts, *, nb, tt, pos0, grid, row_block, row_map, st_map):
    lead = x.shape[:-1]
    outs_rows = lambda w: jax.ShapeDtypeStruct(lead + (w,), F32)
    rb = lambda w: pl.BlockSpec(row_block + (w,), row_map)
    cz = max(SUBLANES, nb)
    cu = POOL_CARRY * nb
    shift_block = (1, cz, D_SHIFT)
    pool_block = (1, POOL_CARRY, D_POOL) if nb == 1 else (POOL_CARRY, nb, D_POOL)
    shift_spec = pl.BlockSpec(shift_block, st_map)
    pool_spec = pl.BlockSpec(pool_block, st_map if nb == 1 else (lambda b, t: (0, b, 0)))
    in_specs = [rb(D_MODEL), shift_spec, pool_spec] + [_const_spec(w.shape) for w in wts]
    out_shape = [outs_rows(D_A)] * 7 + [outs_rows(D_MODEL)] * 2 + [
        jax.ShapeDtypeStruct(shift_in.shape, F32), jax.ShapeDtypeStruct(pool_in.shape, F32)]
    out_specs = [rb(D_A)] * 7 + [rb(D_MODEL)] * 2 + [shift_spec, pool_spec]
    return pl.pallas_call(
        functools.partial(_prep_kernel, nb=nb, tt=tt, pos0=pos0),
        grid=grid, in_specs=in_specs, out_specs=out_specs, out_shape=out_shape,
        scratch_shapes=[pltpu.VMEM((cz, D_SHIFT), F32), pltpu.VMEM((cu, D_POOL), F32)],
        compiler_params=pltpu.CompilerParams(dimension_semantics=("arbitrary", "arbitrary"),
                                             vmem_limit_bytes=VMEM_LIMIT),
        name="prep",
    )(x, shift_in, pool_in, *wts)


def _scan_masks(steps):
    n = 2 * CHUNK
    ri = lax.broadcasted_iota(jnp.int32, (n, n), 0)
    ci = lax.broadcasted_iota(jnp.int32, (n, n), 1)
    blk = lambda s: _div_pow2(ri, s) == _div_pow2(ci, s)
    same_seq = blk(CHUNK) & blk(steps)
    strict = same_seq & (ri > ci)
    incl = same_seq & (ri >= ci)
    levels = []
    size = SUBLANES
    while size < steps:
        size *= 2
        levels.append(strict & blk(size) & ~blk(size // 2))
    base = strict & blk(SUBLANES)
    eye = ri == ci
    return same_seq, strict, incl, base, levels, eye


def _tri_inverse(a_strict, base, levels, eye):
    ad = jnp.where(base, a_strict, 0.0)
    p2 = _dot(ad, ad)
    t = jnp.where(eye, 1.0, ad)
    t = t + _dot(t, p2)
    t = t + _dot(t, _dot(p2, p2))
    for lvl in levels:
        t = t + _dot(t, _dot(jnp.where(lvl, a_strict, 0.0), t))
    return t


def _scan_kernel(r_ref, lw_ref, k_ref, v_ref, a_ref, b_ref, h0_ref, y_ref, hout_ref, h_ref,
                 *, steps, n_chunks):
    nseq = CHUNK // steps
    t = pl.program_id(1)

    @pl.when(t == 0)
    def _():
        h_ref[...] = h0_ref[...].reshape(h_ref.shape)

    same_seq, strict, incl, base, levels, eye = _scan_masks(steps)
    sel_cs = incl[:CHUNK, :CHUNK].astype(BF16)
    sel_cl = same_seq[:CHUNK, :CHUNK].astype(BF16)
    lane = lax.broadcasted_iota(jnp.int32, (1, LANES), 1)
    m0 = (lane < HEAD_DIM).astype(F32)
    m1 = 1.0 - m0
    stack = lambda x: jnp.concatenate([x * m0, x * m1], axis=0)

    def chunk_body(c, carry):
        rows = pl.ds(pl.multiple_of(c * CHUNK, CHUNK), CHUNK)
        for p in range(PAIRS):
            lanes = pl.ds(p * LANES, LANES)
            ld = lambda ref: ref[0, rows, lanes] if len(ref.shape) == 3 else ref[rows, lanes]
            rp, lw, kp, vp, ap, bp = (ld(x) for x in (r_ref, lw_ref, k_ref, v_ref, a_ref, b_ref))
            cs = _dot_sel(sel_cs, lw)
            cl = _dot_sel(sel_cl, lw)
            e_cs = jnp.exp(cs)
            e_ncs = jnp.exp(-cs)
            e_cl = jnp.exp(cl - cs)
            at = stack(ap * jnp.exp(cs - lw))
            rt = stack(rp * e_cs)
            bt = bp * e_ncs
            kt = kp * e_ncs
            bh = stack(bp * e_cl)
            kh = stack(kp * e_cl)
            vs = stack(vp)
            aa = _dot_nt(jnp.concatenate([at, rt], axis=0), jnp.concatenate([bt, bt, kt, kt], axis=0))
            n = 2 * CHUNK
            a_ab = jnp.where(strict, aa[:n, :n], 0.0)
            a_ak = jnp.where(strict, aa[:n, n:], 0.0)
            a_rb = jnp.where(incl, aa[n:, :n], 0.0)
            a_rk = jnp.where(incl, aa[n:, n:], 0.0)
            tinv = _tri_inverse(a_ab, base, levels, eye)
            tx = _dot(tinv, jnp.concatenate([at, _dot(a_ak, vs)], axis=1))
            rx = _dot(a_rb, tx)
            ah = tx[:, :LANES]
            w1 = tx[:, LANES:]
            rh = rt + rx[:, :LANES]
            y1 = rx[:, LANES:] + _dot(a_rk, vs)
            gam = jnp.exp(cl)
            for s in range(nseq):
                pick = lambda x: jnp.concatenate(
                    [x[s * steps:(s + 1) * steps], x[CHUNK + s * steps:CHUNK + (s + 1) * steps]], axis=0)
                h = h_ref[s, p]
                uh = _dot(jnp.concatenate([pick(ah), pick(rh)], axis=0), h)
                u = uh[:2 * steps] + pick(w1)
                ys = uh[2 * steps:] + pick(y1)
                y = ys[:steps] + ys[steps:]
                dg = jnp.where(eye, gam[s * steps:s * steps + 1, :], 0.0)
                h_new = _dot_tn(jnp.concatenate([pick(bh), pick(kh), dg], axis=0),
                                jnp.concatenate([u, pick(vs), h], axis=0))
                h_ref[s, p] = h_new
                if len(y_ref.shape) == 3:
                    y_ref[0, pl.ds(pl.multiple_of(c * CHUNK, CHUNK), CHUNK), lanes] = y
                else:
                    y_ref[pl.ds(s * steps, steps), lanes] = y
        return carry

    lax.fori_loop(0, n_chunks, chunk_body, 0)
    hout_ref[...] = h_ref[...].reshape(hout_ref.shape)


def _scan_call(vecs, h0, *, steps, n_chunks, grid, row_block, row_map):
    nseq = CHUNK // steps
    rb = pl.BlockSpec(row_block + (D_A,), row_map)
    hb = pl.BlockSpec((1, nseq, PAIRS, LANES, LANES), lambda b, t: (b, 0, 0, 0, 0))
    return pl.pallas_call(
        functools.partial(_scan_kernel, steps=steps, n_chunks=n_chunks),
        grid=grid, in_specs=[rb] * 6 + [hb], out_specs=[rb, hb],
        out_shape=[jax.ShapeDtypeStruct(vecs[0].shape, F32), jax.ShapeDtypeStruct(h0.shape, F32)],
        scratch_shapes=[pltpu.VMEM((nseq, PAIRS, LANES, LANES), F32)],
        compiler_params=pltpu.CompilerParams(dimension_semantics=("arbitrary", "arbitrary"),
                                             vmem_limit_bytes=VMEM_LIMIT),
        name="scan",
    )(*vecs, h0)


def _post_kernel(y_ref, r_ref, k_ref, v_ref, g_ref, ga_ref, gbb_ref, x_ref, ones_ref, rk_ref, lng_ref,
                 lnb_ref, waup_ref, wo_ref, n2_ref, w1_ref, w2_ref, fn_ref, o_ref, *, final):
    ones = ones_ref[...]
    y = y_ref[...]
    mu = _seg_sum(y, ones) * (1.0 / HEAD_DIM)
    d = y - mu
    var = _seg_sum(d * d, ones) * (1.0 / HEAD_DIM)
    yn = d * lax.rsqrt(var + GN_EPS) * lng_ref[...] + lnb_ref[...]
    bonus = _seg_sum(r_ref[...] * k_ref[...] * rk_ref[...], ones) * v_ref[...]
    a_out = _dot((yn + bonus) * g_ref[...], waup_ref[...])
    merged = ga_ref[...] * a_out + gbb_ref[...]
    x1 = x_ref[...] + _dot(merged, wo_ref[...])
    f = jnp.maximum(_dot(_rms(x1, n2_ref[...]), w1_ref[...]), 0.0)
    x2 = x1 + _dot(f * f, w2_ref[...])
    o_ref[...] = _rms(x2, fn_ref[...]) if final else x2


def _post_call(rows_in, wts, *, final, tm):
    n = rows_in[0].shape[0]
    rb = lambda a: pl.BlockSpec((tm, a.shape[-1]), lambda i: (i, 0))
    return pl.pallas_call(
        functools.partial(_post_kernel, final=final),
        grid=(n // tm,),
        in_specs=[rb(a) for a in rows_in] + [_const_spec(w.shape) for w in wts],
        out_specs=pl.BlockSpec((tm, D_MODEL), lambda i: (i, 0)),
        out_shape=jax.ShapeDtypeStruct((n, D_MODEL), F32),
        compiler_params=pltpu.CompilerParams(dimension_semantics=("arbitrary",),
                                             vmem_limit_bytes=VMEM_LIMIT),
        name="post",
    )(*rows_in, *wts)


def _pair_state(s):
    n = s.shape[0]
    h = jnp.swapaxes(s, -1, -2).reshape(n, PAIRS, 2, HEAD_DIM, HEAD_DIM)
    z = jnp.zeros_like(h[:, :, 0])
    top = jnp.concatenate([h[:, :, 0], z], axis=-1)
    bot = jnp.concatenate([z, h[:, :, 1]], axis=-1)
    return jnp.concatenate([top, bot], axis=-2)


def _unpair_state(h):
    n = h.shape[0]
    h0 = h[:, :, :HEAD_DIM, :HEAD_DIM]
    h1 = h[:, :, HEAD_DIM:, HEAD_DIM:]
    s = jnp.stack([h0, h1], axis=2).reshape(n, N_HEADS, HEAD_DIM, HEAD_DIM)
    return jnp.swapaxes(s, -1, -2)


def _layer_weights(l, w_in, mu_shift, decay0, w_decay2, a0, w_a2, w_g2, k_k, k_a, r_k, ln_x_g, ln_x_b,
                   w_a_up, w_pool, pool_scale, w_b_up, w_o, norm1_g, norm2_g, w_ff1, w_ff2, ones_bd):
    row = lambda a: a[l].reshape(1, -1)
    zpad = jnp.zeros((LORA_WA // 2, D_A), F32)
    prep = [row(norm1_g), w_in[l].astype(BF16), row(mu_shift), row(decay0),
            jnp.concatenate([w_decay2[l], zpad], axis=0).astype(BF16), row(a0),
            jnp.concatenate([zpad, w_a2[l]], axis=0).astype(BF16), w_g2[l].astype(BF16),
            row(k_k), row(k_a), ones_bd, w_pool[l].astype(BF16), row(pool_scale), w_b_up[l].astype(BF16)]
    post = [ones_bd, row(r_k), row(ln_x_g), row(ln_x_b), w_a_up[l].astype(BF16), w_o[l].astype(BF16),
            row(norm2_g), w_ff1[l].astype(BF16), w_ff2[l].astype(BF16)]
    return prep, post


def kernel(x_prompt, x_sample, state_shift, state_pool, state_wkv, norm1_g, w_in, mu_shift, decay0,
           w_decay2, a0, w_a2, w_g2, k_k, k_a, r_k, ln_x_g, ln_x_b, w_a_up, w_pool, pool_scale, w_b_up,
           w_o, norm2_g, w_ff1, w_ff2, final_norm_g):
    depth = w_in.shape[0]
    bp, tp, _ = x_prompt.shape
    bs, ts, _ = x_sample.shape
    past_len = 16384
    tt_p = 256
    nb_s = 32
    ts_p = 512
    seg = jnp.arange(D_A) // HEAD_DIM
    ones_bd = (seg[:, None] == seg[None, :]).astype(BF16)
    fn = final_norm_g.reshape(1, -1)

    xp = x_prompt
    xs = x_sample.reshape(bs * ts, D_MODEL)
    outs = {k: [] for k in ("p_shift", "p_pool", "p_wkv", "s_shift", "s_pool", "s_wkv")}
    for l in range(depth):
        prep_w, post_w = _layer_weights(
            l, w_in, mu_shift, decay0, w_decay2, a0, w_a2, w_g2, k_k, k_a, r_k, ln_x_g, ln_x_b, w_a_up,
            w_pool, pool_scale, w_b_up, w_o, norm1_g, norm2_g, w_ff1, w_ff2, ones_bd)
        final = l == depth - 1
        post_w = post_w + [fn]

        res = _prep_call(
            xp, jnp.zeros((bp, SUBLANES, D_SHIFT), F32), jnp.zeros((bp, POOL_CARRY, D_POOL), F32), prep_w,
            nb=1, tt=tt_p, pos0=0, grid=(bp, tp // tt_p), row_block=(1, tt_p),
            row_map=lambda b, t: (b, t, 0), st_map=lambda b, t: (b, 0, 0))
        r, lw, k, v, a, b, g, ga, gbb, sh, pool = res
        outs["p_shift"].append(sh[:, SUBLANES - 1])
        outs["p_pool"].append(pool[:, 1:])
        y, hfin = _scan_call(
            (r, lw, k, v, a, b), jnp.zeros((bp, 1, PAIRS, LANES, LANES), F32), steps=CHUNK,
            n_chunks=ts_p // CHUNK, grid=(bp, tp // ts_p), row_block=(1, ts_p),
            row_map=lambda b, t: (b, t, 0))
        outs["p_wkv"].append(_unpair_state(hfin[:, 0]))
        flat = lambda z: z.reshape(bp * tp, z.shape[-1])
        xp = _post_call([flat(z) for z in (y, r, k, v, g, ga, gbb, xp)], post_w, final=final,
                        tm=256).reshape(bp, tp, D_MODEL)

        xs_tm = jnp.swapaxes(xs.reshape(bs, ts, D_MODEL), 0, 1)
        pool_tm = jnp.swapaxes(jnp.pad(state_pool[l], ((0, 0), (1, 0), (0, 0))), 0, 1)
        res = _prep_call(
            xs_tm, state_shift[l][None], pool_tm, prep_w,
            nb=nb_s, tt=ts, pos0=past_len, grid=(bs // nb_s, 1), row_block=(ts, nb_s),
            row_map=lambda b, t: (0, b, 0), st_map=lambda b, t: (0, b, 0))
        sm = lambda z: jnp.swapaxes(z, 0, 1).reshape(bs * ts, z.shape[-1])
        r, lw, k, v, a, b, g, ga, gbb = (sm(z) for z in res[:9])
        outs["s_shift"].append(res[9][0])
        outs["s_pool"].append(jnp.swapaxes(res[10], 0, 1)[:, 1:])
        nseq = CHUNK // ts
        h0 = _pair_state(state_wkv[l]).reshape(bs // nseq, nseq, PAIRS, LANES, LANES)
        y, hfin = _scan_call(
            (r, lw, k, v, a, b), h0, steps=ts, n_chunks=1, grid=(bs // nseq, 1), row_block=(CHUNK,),
            row_map=lambda b, t: (b, 0))
        outs["s_wkv"].append(_unpair_state(hfin.reshape(bs, PAIRS, LANES, LANES)))
        xs = _post_call([y, r, k, v, g, ga, gbb, xs], post_w, final=final, tm=256)

    st = lambda key: jnp.stack(outs[key])
    return (xp, xs.reshape(bs, ts, D_MODEL), st("p_shift"), st("p_pool"), st("p_wkv"),
            st("s_shift"), st("s_pool"), st("s_wkv"))
```

```python
import functools

import jax
import jax.numpy as jnp
from jax import lax
from jax.experimental import pallas as pl
from jax.experimental.pallas import tpu as pltpu

F32 = jnp.float32
BF16 = jnp.bfloat16

D_MODEL = 1024
HEAD_DIM = 64
D_A = 512
N_HEADS = 8
LORA_WA = 128
LORA_G = 128
D_SHIFT = 3 * D_A + LORA_WA + LORA_G
D_POOL = 512
POOL_WINDOWS = (2, 4, 8, 16)
POOL_GD = 128
POOL_CARRY = 16
D_FF = 4096
RMS_EPS = 1e-6
GN_EPS = HEAD_DIM * 1e-5
L2_EPS = 1e-12

LANES = 128
SUBLANES = 8
CHUNK = 64
PAIRS = D_A // LANES
VMEM_LIMIT = 56 * 1024 * 1024


def _dot(a, b):
    return jnp.dot(a.astype(BF16), b.astype(BF16), preferred_element_type=F32)


def _dot_nt(a, b):
    return lax.dot_general(a.astype(BF16), b.astype(BF16), (((1,), (1,)), ((), ())),
                           preferred_element_type=F32)


def _dot_tn(a, b):
    return lax.dot_general(a.astype(BF16), b.astype(BF16), (((0,), (0,)), ((), ())),
                           preferred_element_type=F32)


def _split3(x):
    hi = x.astype(BF16)
    r1 = x - hi.astype(F32)
    mid = r1.astype(BF16)
    lo = (r1 - mid.astype(F32)).astype(BF16)
    return hi, mid, lo


def _dot_sel(sel, x):
    hi, mid, lo = _split3(x)
    d = lambda p: jnp.dot(sel, p, preferred_element_type=F32)
    return d(hi) + d(mid) + d(lo)


def _seg_sum(x, ones_bd):
    hi, mid, lo = _split3(x)
    d = lambda p: jnp.dot(p, ones_bd, preferred_element_type=F32)
    return d(hi) + d(mid) + d(lo)


def _div_pow2(x, d):
    assert d & (d - 1) == 0
    return jnp.right_shift(x, d.bit_length() - 1)


def _rms(x, g):
    return x * lax.rsqrt(jnp.mean(x * x, axis=-1, keepdims=True) + RMS_EPS) * g


def _sigmoid(x):
    return 1.0 / (1.0 + jnp.exp(-x))


def _softplus(x):
    return jnp.maximum(x, 0.0) + jnp.log(1.0 + jnp.exp(-jnp.abs(x)))


def _prep_kernel(x_ref, shift_ref, pool_ref, n1_ref, win_ref, mu_ref, dec0_ref, wdec_ref, a0_ref,
                 wa_ref, wg_ref, kk_ref, ka_ref, ones_ref, wpool_ref, pscale_ref, wbup_ref,
                 r_ref, lw_ref, k_ref, v_ref, a_ref, b_ref, g_ref, ga_ref, gbb_ref,
                 shift_out_ref, pool_out_ref, cz_ref, cu_ref, *, nb, tt, pos0):
    tm = tt * nb
    cz = cz_ref.shape[0]
    cu = cu_ref.shape[0]
    t = pl.program_id(1)

    @pl.when(t == 0)
    def _():
        cz_ref[...] = shift_ref[...].reshape(cz_ref.shape)
        cu_ref[...] = pool_ref[...].reshape(cu_ref.shape)

    x = x_ref[...].reshape(tm, D_MODEL)
    hb = _rms(x, n1_ref[...]).astype(BF16)

    z_rw = jnp.dot(hb, win_ref[:, 0:D_SHIFT], preferred_element_type=F32)
    ext = jnp.concatenate([cz_ref[...], z_rw], axis=0)
    z_prev = pltpu.roll(ext, nb, 0)[cz:]
    zs = z_rw + (z_prev - z_rw) * mu_ref[...]
    new_cz = z_rw[tm - cz:]
    cz_ref[...] = new_cz
    shift_out_ref[...] = new_cz.reshape(shift_out_ref.shape)

    r = zs[:, 0:D_A]
    k = zs[:, D_A:2 * D_A]
    v = zs[:, 2 * D_A:3 * D_A]
    lwa = zs[:, 3 * D_A:3 * D_A + LORA_WA]
    lg = zs[:, 3 * D_A + LORA_WA:D_SHIFT]
    w_log = -_softplus(-(dec0_ref[...] + _dot(jnp.tanh(lwa), wdec_ref[...]))) - 0.5
    a_in = _sigmoid(a0_ref[...] + _dot(lwa, wa_ref[...]))
    kk = k * kk_ref[...]
    kk = kk / jnp.maximum(jnp.sqrt(_seg_sum(kk * kk, ones_ref[...])), L2_EPS)
    osh = r_ref.shape
    r_ref[...] = r.reshape(osh)
    lw_ref[...] = (-jnp.exp(w_log)).reshape(osh)
    k_ref[...] = (k * (1.0 + (a_in - 1.0) * ka_ref[...])).reshape(osh)
    v_ref[...] = v.reshape(osh)
    a_ref[...] = (-kk).reshape(osh)
    b_ref[...] = (kk * a_in).reshape(osh)
    g_ref[...] = _dot(_sigmoid(lg), wg_ref[...]).reshape(osh)

    u = jnp.dot(hb, win_ref[:, D_SHIFT:D_SHIFT + D_POOL], preferred_element_type=F32)
    extu = jnp.concatenate([cu_ref[...], u], axis=0)
    new_cu = extu[tm:]
    cu_ref[...] = new_cu
    pool_out_ref[...] = new_cu.reshape(pool_out_ref.shape)
    row = lax.broadcasted_iota(jnp.int32, (tm, 1), 0)
    step = _div_pow2(row, nb)
    pos = pos0 + t * tt + step
    parts = []
    for gi, win in enumerate(POOL_WINDOWS):
        sl = slice(gi * POOL_GD, (gi + 1) * POOL_GD)
        s = extu[:, sl]
        span = 1
        while span < win:
            s = s + pltpu.roll(s, span * nb, 0)
            span *= 2
        cnt = jnp.minimum(pos + 1, win).astype(F32)
        p = s[cu:] / cnt - u[:, sl]
        parts.append(_dot(p, wpool_ref[gi]))
    pooled = jnp.concatenate(parts, axis=-1) * pscale_ref[...]
    b_out = _dot(pooled, wbup_ref[...])

    o2 = ga_ref.shape
    c0 = D_SHIFT + D_POOL
    ga_ref[...] = _sigmoid(jnp.dot(hb, win_ref[:, c0:c0 + D_MODEL],
                                   preferred_element_type=F32)).reshape(o2)
    gate_b = _sigmoid(jnp.dot(hb, win_ref[:, c0 + D_MODEL:c0 + 2 * D_MODEL],
                              preferred_element_type=F32))
    gbb_ref[...] = (gate_b * b_out).reshape(o2)


def _const_spec(shape):
    nd = len(shape)
    return pl.BlockSpec(shape, lambda *_: (0,) * nd, pipeline_mode=pl.Buffered(1))


def _prep_call(x, shift_in, pool_in, wts, *, nb, tt, pos0, grid, row_block, row_map, st_map):
    lead = x.shape[:-1]
    outs_rows = lambda w: jax.ShapeDtypeStruct(lead + (w,), F32)
    rb = lambda w: pl.BlockSpec(row_block + (w,), row_map)
    cz = max(SUBLANES, nb)
    cu = POOL_CARRY * nb
    shift_block = (1, cz, D_SHIFT)
    pool_block = (1, POOL_CARRY, D_POOL) if nb == 1 else (POOL_CARRY, nb, D_POOL)
    shift_spec = pl.BlockSpec(shift_block, st_map)
    pool_spec = pl.BlockSpec(pool_block, st_map if nb == 1 else (lambda b, t: (0, b, 0)))
    in_specs = [rb(D_MODEL), shift_spec, pool_spec] + [_const_spec(w.shape) for w in wts]
    out_shape = [outs_rows(D_A)] * 7 + [outs_rows(D_MODEL)] * 2 + [
        jax.ShapeDtypeStruct(shift_in.shape, F32), jax.ShapeDtypeStruct(pool_in.shape, F32)]
    out_specs = [rb(D_A)] * 7 + [rb(D_MODEL)] * 2 + [shift_spec, pool_spec]
    return pl.pallas_call(
        functools.partial(_prep_kernel, nb=nb, tt=tt, pos0=pos0),
        grid=grid, in_specs=in_specs, out_specs=out_specs, out_shape=out_shape,
        scratch_shapes=[pltpu.VMEM((cz, D_SHIFT), F32), pltpu.VMEM((cu, D_POOL), F32)],
        compiler_params=pltpu.CompilerParams(dimension_semantics=("arbitrary", "arbitrary"),
                                             vmem_limit_bytes=VMEM_LIMIT),
        name="prep",
    )(x, shift_in, pool_in, *wts)


def _scan_masks(steps):
    n = 2 * CHUNK
    ri = lax.broadcasted_iota(jnp.int32, (n, n), 0)
    ci = lax.broadcasted_iota(jnp.int32, (n, n), 1)
    blk = lambda s: _div_pow2(ri, s) == _div_pow2(ci, s)
    same_seq = blk(CHUNK) & blk(steps)
    strict = same_seq & (ri > ci)
    incl = same_seq & (ri >= ci)
    levels = []
    size = SUBLANES
    while size < steps:
        size *= 2
        levels.append(strict & blk(size) & ~blk(size // 2))
    base = strict & blk(SUBLANES)
    eye = ri == ci
    return same_seq, strict, incl, base, levels, eye


def _bdot(a, b):
    return lax.dot_general(a.astype(BF16), b.astype(BF16), (((2,), (1,)), ((0,), (0,))),
                           preferred_element_type=F32)


def _bdot_nt(a, b):
    return lax.dot_general(a.astype(BF16), b.astype(BF16), (((2,), (2,)), ((0,), (0,))),
                           preferred_element_type=F32)


def _tri_inverse(a_strict, base, levels, eye):
    ad = jnp.where(base, a_strict, 0.0)
    p2 = _bdot(ad, ad)
    t = jnp.where(eye, 1.0, ad)
    t = t + _bdot(t, p2)
    t = t + _bdot(t, _bdot(p2, p2))
    for lvl in levels:
        t = t + _bdot(t, _bdot(jnp.where(lvl, a_strict, 0.0), t))
    return t


def _scan_kernel(r_ref, lw_ref, k_ref, v_ref, a_ref, b_ref, h0_ref, y_ref, hout_ref, h_ref,
                 *, steps, n_chunks):
    nseq = CHUNK // steps
    t = pl.program_id(1)

    @pl.when(t == 0)
    def _():
        h_ref[...] = h0_ref[...].reshape(h_ref.shape)

    same_seq, strict, incl, base, levels, eye = _scan_masks(steps)
    sel = jnp.concatenate([incl[:CHUNK, :CHUNK], same_seq[:CHUNK, :CHUNK]], axis=0).astype(BF16)
    lane = lax.broadcasted_iota(jnp.int32, (1, LANES), 1)
    m0 = (lane < HEAD_DIM).astype(F32)
    m1 = 1.0 - m0
    rows = n_chunks * CHUNK

    ld = lambda ref: ref[...].reshape(rows, D_A)
    r, lw, k, v, a, b = (ld(x) for x in (r_ref, lw_ref, k_ref, v_ref, a_ref, b_ref))
    cc = [_dot_sel(sel, lw[c * CHUNK:(c + 1) * CHUNK]) for c in range(n_chunks)]
    cs = jnp.concatenate([x[:CHUNK] for x in cc], axis=0)
    cl = jnp.concatenate([x[CHUNK:] for x in cc], axis=0)
    e_cs = jnp.exp(cs)
    e_ncs = jnp.exp(-cs)
    e_cl = jnp.exp(cl - cs)
    gam = jnp.exp(cl)

    def inst(x):
        x3 = x.reshape(n_chunks, CHUNK, D_A)
        return jnp.concatenate([x3[:, :, p * LANES:(p + 1) * LANES] for p in range(PAIRS)], axis=0)

    stack = lambda x: jnp.concatenate([x * m0, x * m1], axis=1)
    twice = lambda x: jnp.concatenate([x, x], axis=1)
    at = stack(inst(a * jnp.exp(cs - lw)))
    rt = stack(inst(r * e_cs))
    bt = twice(inst(b * e_ncs))
    kt = twice(inst(k * e_ncs))
    bh = stack(inst(b * e_cl))
    kh = stack(inst(k * e_cl))
    vs = stack(inst(v))
    a_ab = jnp.where(strict, _bdot_nt(at, bt), 0.0)
    a_ak = jnp.where(strict, _bdot_nt(at, kt), 0.0)
    a_rb = jnp.where(incl, _bdot_nt(rt, bt), 0.0)
    a_rk = jnp.where(incl, _bdot_nt(rt, kt), 0.0)
    tinv = _tri_inverse(a_ab, base, levels, eye)
    ah = _bdot(tinv, at)
    w1 = _bdot(tinv, _bdot(a_ak, vs))
    rh = rt + _bdot(a_rb, ah)
    y1 = _bdot(a_rb, w1) + _bdot(a_rk, vs)

    hs = [[h_ref[s, p] for p in range(PAIRS)] for s in range(nseq)]
    for c in range(n_chunks):
        for p in range(PAIRS):
            g = p * n_chunks + c
            for s in range(nseq):
                pick = lambda x: jnp.concatenate(
                    [x[g, s * steps:(s + 1) * steps], x[g, CHUNK + s * steps:CHUNK + (s + 1) * steps]],
                    axis=0)
                h = hs[s][p]
                uh = _dot(jnp.concatenate([pick(ah), pick(rh)], axis=0), h)
                u = uh[:2 * steps] + pick(w1)
                ys = uh[2 * steps:] + pick(y1)
                y = ys[:steps] + ys[steps:]
                r0 = c * CHUNK + s * steps
                dg = jnp.where(eye, gam[r0:r0 + 1, p * LANES:(p + 1) * LANES], 0.0)
                hs[s][p] = _dot_tn(jnp.concatenate([pick(bh), pick(kh), dg], axis=0),
                                   jnp.concatenate([u, pick(vs), h], axis=0))
                if len(y_ref.shape) == 3:
                    y_ref[0, r0:r0 + steps, p * LANES:(p + 1) * LANES] = y
                else:
                    y_ref[r0:r0 + steps, p * LANES:(p + 1) * LANES] = y
    for s in range(nseq):
        for p in range(PAIRS):
            h_ref[s, p] = hs[s][p]
    hout_ref[...] = h_ref[...].reshape(hout_ref.shape)


def _scan_call(vecs, h0, *, steps, n_chunks, grid, row_block, row_map):
    nseq = CHUNK // steps
    rb = pl.BlockSpec(row_block + (D_A,), row_map)
    hb = pl.BlockSpec((1, nseq, PAIRS, LANES, LANES), lambda b, t: (b, 0, 0, 0, 0))
    return pl.pallas_call(
        functools.partial(_scan_kernel, steps=steps, n_chunks=n_chunks),
        grid=grid, in_specs=[rb] * 6 + [hb], out_specs=[rb, hb],
        out_shape=[jax.ShapeDtypeStruct(vecs[0].shape, F32), jax.ShapeDtypeStruct(h0.shape, F32)],
        scratch_shapes=[pltpu.VMEM((nseq, PAIRS, LANES, LANES), F32)],
        compiler_params=pltpu.CompilerParams(dimension_semantics=("arbitrary", "arbitrary"),
                                             vmem_limit_bytes=VMEM_LIMIT),
        name="scan",
    )(*vecs, h0)


def _post_kernel(y_ref, r_ref, k_ref, v_ref, g_ref, ga_ref, gbb_ref, x_ref, ones_ref, rk_ref, lng_ref,
                 lnb_ref, waup_ref, wo_ref, n2_ref, w1_ref, w2_ref, fn_ref, o_ref, *, final):
    ones = ones_ref[...]
    y = y_ref[...]
    mu = _seg_sum(y, ones) * (1.0 / HEAD_DIM)
    d = y - mu
    var = _seg_sum(d * d, ones) * (1.0 / HEAD_DIM)
    yn = d * lax.rsqrt(var + GN_EPS) * lng_ref[...] + lnb_ref[...]
    bonus = _seg_sum(r_ref[...] * k_ref[...] * rk_ref[...], ones) * v_ref[...]
    a_out = _dot((yn + bonus) * g_ref[...], waup_ref[...])
    merged = ga_ref[...] * a_out + gbb_ref[...]
    x1 = x_ref[...] + _dot(merged, wo_ref[...])
    f = jnp.maximum(_dot(_rms(x1, n2_ref[...]), w1_ref[...]), 0.0)
    x2 = x1 + _dot(f * f, w2_ref[...])
    o_ref[...] = _rms(x2, fn_ref[...]) if final else x2


def _post_call(rows_in, wts, *, final, tm):
    n = rows_in[0].shape[0]
    rb = lambda a: pl.BlockSpec((tm, a.shape[-1]), lambda i: (i, 0))
    return pl.pallas_call(
        functools.partial(_post_kernel, final=final),
        grid=(n // tm,),
        in_specs=[rb(a) for a in rows_in] + [_const_spec(w.shape) for w in wts],
        out_specs=pl.BlockSpec((tm, D_MODEL), lambda i: (i, 0)),
        out_shape=jax.ShapeDtypeStruct((n, D_MODEL), F32),
        compiler_params=pltpu.CompilerParams(dimension_semantics=("arbitrary",),
                                             vmem_limit_bytes=VMEM_LIMIT),
        name="post",
    )(*rows_in, *wts)


def _pair_state(s):
    n = s.shape[0]
    h = jnp.swapaxes(s, -1, -2).reshape(n, PAIRS, 2, HEAD_DIM, HEAD_DIM)
    z = jnp.zeros_like(h[:, :, 0])
    top = jnp.concatenate([h[:, :, 0], z], axis=-1)
    bot = jnp.concatenate([z, h[:, :, 1]], axis=-1)
    return jnp.concatenate([top, bot], axis=-2)


def _unpair_state(h):
    n = h.shape[0]
    h0 = h[:, :, :HEAD_DIM, :HEAD_DIM]
    h1 = h[:, :, HEAD_DIM:, HEAD_DIM:]
    s = jnp.stack([h0, h1], axis=2).reshape(n, N_HEADS, HEAD_DIM, HEAD_DIM)
    return jnp.swapaxes(s, -1, -2)


def _layer_weights(l, w_in, mu_shift, decay0, w_decay2, a0, w_a2, w_g2, k_k, k_a, r_k, ln_x_g, ln_x_b,
                   w_a_up, w_pool, pool_scale, w_b_up, w_o, norm1_g, norm2_g, w_ff1, w_ff2, ones_bd):
    row = lambda a: a[l].reshape(1, -1)
    zpad = jnp.zeros((LORA_WA // 2, D_A), F32)
    prep = [row(norm1_g), w_in[l].astype(BF16), row(mu_shift), row(decay0),
            jnp.concatenate([w_decay2[l], zpad], axis=0).astype(BF16), row(a0),
            jnp.concatenate([zpad, w_a2[l]], axis=0).astype(BF16), w_g2[l].astype(BF16),
            row(k_k), row(k_a), ones_bd, w_pool[l].astype(BF16), row(pool_scale), w_b_up[l].astype(BF16)]
    post = [ones_bd, row(r_k), row(ln_x_g), row(ln_x_b), w_a_up[l].astype(BF16), w_o[l].astype(BF16),
            row(norm2_g), w_ff1[l].astype(BF16), w_ff2[l].astype(BF16)]
    return prep, post


def kernel(x_prompt, x_sample, state_shift, state_pool, state_wkv, norm1_g, w_in, mu_shift, decay0,
           w_decay2, a0, w_a2, w_g2, k_k, k_a, r_k, ln_x_g, ln_x_b, w_a_up, w_pool, pool_scale, w_b_up,
           w_o, norm2_g, w_ff1, w_ff2, final_norm_g):
    depth = w_in.shape[0]
    bp, tp, _ = x_prompt.shape
    bs, ts, _ = x_sample.shape
    past_len = 16384
    tt_p = 256
    nb_s = 32
    ts_p = 256
    seg = jnp.arange(D_A) // HEAD_DIM
    ones_bd = (seg[:, None] == seg[None, :]).astype(BF16)
    fn = final_norm_g.reshape(1, -1)

    xp = x_prompt
    xs = x_sample.reshape(bs * ts, D_MODEL)
    outs = {k: [] for k in ("p_shift", "p_pool", "p_wkv", "s_shift", "s_pool", "s_wkv")}
    for l in range(depth):
        prep_w, post_w = _layer_weights(
            l, w_in, mu_shift, decay0, w_decay2, a0, w_a2, w_g2, k_k, k_a, r_k, ln_x_g, ln_x_b, w_a_up,
            w_pool, pool_scale, w_b_up, w_o, norm1_g, norm2_g, w_ff1, w_ff2, ones_bd)
        final = l == depth - 1
        post_w = post_w + [fn]

        res = _prep_call(
            xp, jnp.zeros((bp, SUBLANES, D_SHIFT), F32), jnp.zeros((bp, POOL_CARRY, D_POOL), F32), prep_w,
            nb=1, tt=tt_p, pos0=0, grid=(bp, tp // tt_p), row_block=(1, tt_p),
            row_map=lambda b, t: (b, t, 0), st_map=lambda b, t: (b, 0, 0))
        r, lw, k, v, a, b, g, ga, gbb, sh, pool = res
        outs["p_shift"].append(sh[:, SUBLANES - 1])
        outs["p_pool"].append(pool[:, 1:])
        y, hfin = _scan_call(
            (r, lw, k, v, a, b), jnp.zeros((bp, 1, PAIRS, LANES, LANES), F32), steps=CHUNK,
            n_chunks=ts_p // CHUNK, grid=(bp, tp // ts_p), row_block=(1, ts_p),
            row_map=lambda b, t: (b, t, 0))
        outs["p_wkv"].append(_unpair_state(hfin[:, 0]))
        flat = lambda z: z.reshape(bp * tp, z.shape[-1])
        xp = _post_call([flat(z) for z in (y, r, k, v, g, ga, gbb, xp)], post_w, final=final,
                        tm=256).reshape(bp, tp, D_MODEL)

        xs_tm = jnp.swapaxes(xs.reshape(bs, ts, D_MODEL), 0, 1)
        pool_tm = jnp.swapaxes(jnp.pad(state_pool[l], ((0, 0), (1, 0), (0, 0))), 0, 1)
        res = _prep_call(
            xs_tm, state_shift[l][None], pool_tm, prep_w,
            nb=nb_s, tt=ts, pos0=past_len, grid=(bs // nb_s, 1), row_block=(ts, nb_s),
            row_map=lambda b, t: (0, b, 0), st_map=lambda b, t: (0, b, 0))
        sm = lambda z: jnp.swapaxes(z, 0, 1).reshape(bs * ts, z.shape[-1])
        r, lw, k, v, a, b, g, ga, gbb = (sm(z) for z in res[:9])
        outs["s_shift"].append(res[9][0])
        outs["s_pool"].append(jnp.swapaxes(res[10], 0, 1)[:, 1:])
        nseq = CHUNK // ts
        h0 = _pair_state(state_wkv[l]).reshape(bs // nseq, nseq, PAIRS, LANES, LANES)
        y, hfin = _scan_call(
            (r, lw, k, v, a, b), h0, steps=ts, n_chunks=1, grid=(bs // nseq, 1), row_block=(CHUNK,),
            row_map=lambda b, t: (b, 0))
        outs["s_wkv"].append(_unpair_state(hfin.reshape(bs, PAIRS, LANES, LANES)))
        xs = _post_call([y, r, k, v, g, ga, gbb, xs], post_w, final=final, tm=256)

    st = lambda key: jnp.stack(outs[key])
    return (xp, xs.reshape(bs, ts, D_MODEL), st("p_shift"), st("p_pool"), st("p_wkv"),
            st("s_shift"), st("s_pool"), st("s_wkv"))
```

```python
import functools

import jax
import jax.numpy as jnp
from jax import lax
from jax.experimental import pallas as pl
from jax.experimental.pallas import tpu as pltpu

F32 = jnp.float32
BF16 = jnp.bfloat16

D_MODEL = 1024
HEAD_DIM = 64
D_A = 512
N_HEADS = 8
LORA_WA = 128
LORA_G = 128
D_SHIFT = 3 * D_A + LORA_WA + LORA_G
D_POOL = 512
POOL_WINDOWS = (2, 4, 8, 16)
POOL_GD = 128
POOL_CARRY = 16
D_FF = 4096
RMS_EPS = 1e-6
GN_EPS = HEAD_DIM * 1e-5
L2_EPS = 1e-12

LANES = 128
SUBLANES = 8
MXU_DIM = 256
CHUNK = 64
PAIRS = D_A // LANES
VMEM_LIMIT = 56 * 1024 * 1024


def _dot(a, b):
    return jnp.dot(a.astype(BF16), b.astype(BF16), preferred_element_type=F32)


def _dot_nt(a, b):
    return lax.dot_general(a.astype(BF16), b.astype(BF16), (((1,), (1,)), ((), ())),
                           preferred_element_type=F32)


def _dot_tn(a, b):
    return lax.dot_general(a.astype(BF16), b.astype(BF16), (((0,), (0,)), ((), ())),
                           preferred_element_type=F32)


def _split3(x):
    hi = x.astype(BF16)
    r1 = x - hi.astype(F32)
    mid = r1.astype(BF16)
    lo = (r1 - mid.astype(F32)).astype(BF16)
    return hi, mid, lo


def _dot_sel(sel, x):
    hi, mid, lo = _split3(x)
    d = lambda p: jnp.dot(sel, p, preferred_element_type=F32)
    return d(hi) + d(mid) + d(lo)


def _seg_sum(x, ones_bd):
    hi = x.astype(BF16)
    lo = (x - hi.astype(F32)).astype(BF16)
    w = ones_bd.shape[0]
    d = lambda p: jnp.concatenate(
        [jnp.dot(p[:, i:i + w], ones_bd, preferred_element_type=F32) for i in range(0, x.shape[1], w)],
        axis=1)
    return d(hi) + d(lo)


def _div_pow2(x, d):
    assert d & (d - 1) == 0
    return jnp.right_shift(x, d.bit_length() - 1)


def _rms(x, g):
    return x * lax.rsqrt(jnp.mean(x * x, axis=-1, keepdims=True) + RMS_EPS) * g


def _sigmoid(x):
    return 1.0 / (1.0 + jnp.exp(-x))


def _softplus(x):
    return jnp.maximum(x, 0.0) + jnp.log(1.0 + jnp.exp(-jnp.abs(x)))


def _prep_kernel(x_ref, shift_ref, pool_ref, n1_ref, win_ref, mu_ref, dec0_ref, wdec_ref, a0_ref,
                 wa_ref, wg_ref, kk_ref, ka_ref, ones_ref, wpool_ref, pscale_ref, wbup_ref,
                 r_ref, lw_ref, k_ref, v_ref, a_ref, b_ref, g_ref, ga_ref, gbb_ref,
                 shift_out_ref, pool_out_ref, cz_ref, cu_ref, *, nb, tt, pos0):
    tm = tt * nb
    cz = cz_ref.shape[0]
    cu = cu_ref.shape[0]
    t = pl.program_id(1)

    @pl.when(t == 0)
    def _():
        cz_ref[...] = shift_ref[...].reshape(cz_ref.shape)
        cu_ref[...] = pool_ref[...].reshape(cu_ref.shape)

    x = x_ref[...].reshape(tm, D_MODEL)
    hb = _rms(x, n1_ref[...]).astype(BF16)

    z_rw = jnp.dot(hb, win_ref[:, 0:D_SHIFT], preferred_element_type=F32)
    ext = jnp.concatenate([cz_ref[...], z_rw], axis=0)
    z_prev = pltpu.roll(ext, nb, 0)[cz:]
    zs = z_rw + (z_prev - z_rw) * mu_ref[...]
    new_cz = z_rw[tm - cz:]
    cz_ref[...] = new_cz
    shift_out_ref[...] = new_cz.reshape(shift_out_ref.shape)

    r = zs[:, 0:D_A]
    k = zs[:, D_A:2 * D_A]
    v = zs[:, 2 * D_A:3 * D_A]
    lwa = zs[:, 3 * D_A:3 * D_A + LORA_WA]
    lg = zs[:, 3 * D_A + LORA_WA:D_SHIFT]
    w_log = -_softplus(-(dec0_ref[...] + _dot(jnp.tanh(lwa), wdec_ref[...]))) - 0.5
    a_in = _sigmoid(a0_ref[...] + _dot(lwa, wa_ref[...]))
    kk = k * kk_ref[...]
    kk = kk / jnp.maximum(jnp.sqrt(_seg_sum(kk * kk, ones_ref[...])), L2_EPS)
    osh = r_ref.shape
    r_ref[...] = r.reshape(osh)
    lw_ref[...] = (-jnp.exp(w_log)).reshape(osh)
    k_ref[...] = (k * (1.0 + (a_in - 1.0) * ka_ref[...])).reshape(osh)
    v_ref[...] = v.reshape(osh)
    a_ref[...] = (-kk).reshape(osh)
    b_ref[...] = (kk * a_in).reshape(osh)
    g_ref[...] = _dot(_sigmoid(lg), wg_ref[...]).reshape(osh)

    u = jnp.dot(hb, win_ref[:, D_SHIFT:D_SHIFT + D_POOL], preferred_element_type=F32)
    extu = jnp.concatenate([cu_ref[...], u], axis=0)
    new_cu = extu[tm:]
    cu_ref[...] = new_cu
    pool_out_ref[...] = new_cu.reshape(pool_out_ref.shape)
    row = lax.broadcasted_iota(jnp.int32, (tm, 1), 0)
    step = _div_pow2(row, nb)
    pos = pos0 + t * tt + step
    parts = []
    for gi, win in enumerate(POOL_WINDOWS):
        sl = slice(gi * POOL_GD, (gi + 1) * POOL_GD)
        s = extu[:, sl]
        span = 1
        while span < win:
            s = s + pltpu.roll(s, span * nb, 0)
            span *= 2
        cnt = jnp.minimum(pos + 1, win).astype(F32)
        p = s[cu:] / cnt - u[:, sl]
        parts.append(_dot(p, wpool_ref[gi]))
    pooled = jnp.concatenate(parts, axis=-1) * pscale_ref[...]
    b_out = _dot(pooled, wbup_ref[...])

    o2 = ga_ref.shape
    c0 = D_SHIFT + D_POOL
    ga_ref[...] = _sigmoid(jnp.dot(hb, win_ref[:, c0:c0 + D_MODEL],
                                   preferred_element_type=F32)).reshape(o2)
    gate_b = _sigmoid(jnp.dot(hb, win_ref[:, c0 + D_MODEL:c0 + 2 * D_MODEL],
                              preferred_element_type=F32))
    gbb_ref[...] = (gate_b * b_out).reshape(o2)


def _const_spec(shape):
    nd = len(shape)
    return pl.BlockSpec(shape, lambda *_: (0,) * nd, pipeline_mode=pl.Buffered(1))


def _prep_call(x, shift_in, pool_in, wts, *, nb, tt, pos0, grid, row_block, row_map, st_map):
    lead = x.shape[:-1]
    outs_rows = lambda w: jax.ShapeDtypeStruct(lead + (w,), F32)
    rb = lambda w: pl.BlockSpec(row_block + (w,), row_map)
    cz = max(SUBLANES, nb)
    cu = POOL_CARRY * nb
    shift_block = (1, cz, D_SHIFT)
    pool_block = (1, POOL_CARRY, D_POOL) if nb == 1 else (POOL_CARRY, nb, D_POOL)
    shift_spec = pl.BlockSpec(shift_block, st_map)
    pool_spec = pl.BlockSpec(pool_block, st_map if nb == 1 else (lambda b, t: (0, b, 0)))
    in_specs = [rb(D_MODEL), shift_spec, pool_spec] + [_const_spec(w.shape) for w in wts]
    out_shape = [outs_rows(D_A)] * 7 + [outs_rows(D_MODEL)] * 2 + [
        jax.ShapeDtypeStruct(shift_in.shape, F32), jax.ShapeDtypeStruct(pool_in.shape, F32)]
    out_specs = [rb(D_A)] * 7 + [rb(D_MODEL)] * 2 + [shift_spec, pool_spec]
    return pl.pallas_call(
        functools.partial(_prep_kernel, nb=nb, tt=tt, pos0=pos0),
        grid=grid, in_specs=in_specs, out_specs=out_specs, out_shape=out_shape,
        scratch_shapes=[pltpu.VMEM((cz, D_SHIFT), F32), pltpu.VMEM((cu, D_POOL), F32)],
        compiler_params=pltpu.CompilerParams(dimension_semantics=("arbitrary", "arbitrary"),
                                             vmem_limit_bytes=VMEM_LIMIT),
        name="prep",
    )(x, shift_in, pool_in, *wts)


def _scan_masks(steps):
    n = 2 * CHUNK
    ri = lax.broadcasted_iota(jnp.int32, (n, n), 0)
    ci = lax.broadcasted_iota(jnp.int32, (n, n), 1)
    blk = lambda s: _div_pow2(ri, s) == _div_pow2(ci, s)
    same_seq = blk(CHUNK) & blk(steps)
    strict = same_seq & (ri > ci)
    incl = same_seq & (ri >= ci)
    levels = []
    size = SUBLANES
    while size < steps:
        size *= 2
        levels.append(strict & blk(size) & ~blk(size // 2))
    base = strict & blk(SUBLANES)
    eye = ri == ci
    return same_seq, strict, incl, base, levels, eye


def _bdot(a, b):
    return lax.dot_general(a.astype(BF16), b.astype(BF16), (((2,), (1,)), ((0,), (0,))),
                           preferred_element_type=F32)


def _bdot_nt(a, b):
    return lax.dot_general(a.astype(BF16), b.astype(BF16), (((2,), (2,)), ((0,), (0,))),
                           preferred_element_type=F32)


def _bdot_tn(a, b):
    return lax.dot_general(a.astype(BF16), b.astype(BF16), (((1,), (1,)), ((0,), (0,))),
                           preferred_element_type=F32)


def _tri_inverse(a_strict, base, levels, eye):
    ad = jnp.where(base, a_strict, 0.0)
    p2 = _bdot(ad, ad)
    t = jnp.where(eye, 1.0, ad)
    t = t + _bdot(t, p2)
    t = t + _bdot(t, _bdot(p2, p2))
    for lvl in levels:
        t = t + _bdot(t, _bdot(jnp.where(lvl, a_strict, 0.0), t))
    return t


def _scan_kernel(r_ref, lw_ref, k_ref, v_ref, a_ref, b_ref, h0_ref, y_ref, hout_ref, h_ref,
                 *, steps, n_chunks):
    nseq = CHUNK // steps
    nst = h_ref.shape[0]
    t = pl.program_id(1)

    @pl.when(t == 0)
    def _():
        zero = jnp.zeros((HEAD_DIM, HEAD_DIM), F32)
        for s in range(nst):
            for p in range(PAIRS):
                top = jnp.concatenate([h0_ref[0, s, p, 0], zero], axis=1)
                bot = jnp.concatenate([zero, h0_ref[0, s, p, 1]], axis=1)
                h_ref[s, p] = jnp.concatenate([top, bot], axis=0)

    same_seq, strict, incl, base, levels, eye = _scan_masks(steps)
    sel = jnp.concatenate([incl[:CHUNK, :CHUNK], same_seq[:CHUNK, :CHUNK]], axis=0).astype(BF16)
    lane = lax.broadcasted_iota(jnp.int32, (1, LANES), 1)
    m0 = (lane < HEAD_DIM).astype(F32)
    m1 = 1.0 - m0
    rows = n_chunks * CHUNK

    ld = lambda ref: ref[...].reshape(rows, D_A)
    r, lw, k, v, a, b = (ld(x) for x in (r_ref, lw_ref, k_ref, v_ref, a_ref, b_ref))
    cc = [_dot_sel(sel, lw[c * CHUNK:(c + 1) * CHUNK]) for c in range(n_chunks)]
    cs = jnp.concatenate([x[:CHUNK] for x in cc], axis=0)
    cl = jnp.concatenate([x[CHUNK:] for x in cc], axis=0)
    e_cs = jnp.exp(cs)
    e_ncs = jnp.exp(-cs)
    e_cl = jnp.exp(cl - cs)
    gam = jnp.exp(cl)

    def inst(x):
        x3 = x.reshape(n_chunks, CHUNK, D_A)
        return jnp.concatenate([x3[:, :, p * LANES:(p + 1) * LANES] for p in range(PAIRS)], axis=0)

    stack = lambda x: jnp.concatenate([x * m0, x * m1], axis=1)
    twice = lambda x: jnp.concatenate([x, x], axis=1)
    at = stack(inst(a * jnp.exp(cs - lw)))
    rt = stack(inst(r * e_cs))
    bt = twice(inst(b * e_ncs))
    kt = twice(inst(k * e_ncs))
    bh = stack(inst(b * e_cl))
    kh = stack(inst(k * e_cl))
    vs = stack(inst(v))
    n = 2 * CHUNK
    aa = _bdot_nt(jnp.concatenate([at, rt], axis=1), jnp.concatenate([bt, kt], axis=1))
    a_ab = jnp.where(strict, aa[:, :n, :n], 0.0)
    a_ak = jnp.where(strict, aa[:, :n, n:], 0.0)
    a_rb = jnp.where(incl, aa[:, n:, :n], 0.0)
    a_rk = jnp.where(incl, aa[:, n:, n:], 0.0)
    tinv = _tri_inverse(a_ab, base, levels, eye)
    tx = _bdot(tinv, jnp.concatenate([at, _bdot(a_ak, vs)], axis=2))
    rx = _bdot(a_rb, tx)
    ah = tx[:, :, :LANES]
    w1 = tx[:, :, LANES:]
    rh = rt + rx[:, :, :LANES]
    y1 = rx[:, :, LANES:] + _bdot(a_rk, vs)

    ng = PAIRS * n_chunks

    def seqsplit(x):
        return jnp.concatenate(
            [jnp.concatenate([x[:, s * steps:(s + 1) * steps],
                              x[:, CHUNK + s * steps:CHUNK + (s + 1) * steps]], axis=1)
             for s in range(nseq)], axis=0)

    bhs = seqsplit(bh)
    mp = _bdot_tn(seqsplit(ah), bhs)
    np_ = _bdot_tn(jnp.concatenate([seqsplit(w1), seqsplit(vs)], axis=1),
                   jnp.concatenate([bhs, seqsplit(kh)], axis=1))
    gam_i = inst(gam)
    gam_m = jnp.concatenate([gam_i[:, s * steps:s * steps + 1] for s in range(nseq)], axis=0)
    order = [(s, p, c) for s in range(nseq) for p in range(PAIRS) for c in range(n_chunks)]
    if nst == nseq:
        take = lambda x, c: jnp.concatenate(
            [x[s * ng + p * n_chunks + c][None] for s in range(nseq) for p in range(PAIRS)], axis=0)
        cur = jnp.concatenate([h_ref[s, p][None] for s in range(nseq) for p in range(PAIRS)], axis=0)
        starts = []
        for c in range(n_chunks):
            starts.append(cur)
            cur = cur * take(gam_m, c) + _bdot(cur, take(mp, c)) + take(np_, c)
        s_start = jnp.concatenate(
            [starts[c][s * PAIRS + p][None] for (s, p, c) in order], axis=0)
        finals = {(s, p): cur[s * PAIRS + p] for s in range(nseq) for p in range(PAIRS)}
    else:
        s_start = jnp.concatenate([h_ref[c * nseq + s, p][None] for (s, p, c) in order], axis=0)
        s_end = s_start * gam_m + _bdot(s_start, mp) + np_
        finals = {(c * nseq + s, p): s_end[i] for i, (s, p, c) in enumerate(order)}
    ys = _bdot_nt(seqsplit(rh), s_start) + seqsplit(y1)
    y = ys[:, :steps] + ys[:, steps:]
    for i, (s, p, c) in enumerate(order):
        r0 = c * CHUNK + s * steps
        if len(y_ref.shape) == 3:
            y_ref[0, r0:r0 + steps, p * LANES:(p + 1) * LANES] = y[i]
        else:
            y_ref[r0:r0 + steps, p * LANES:(p + 1) * LANES] = y[i]
    for (s, p), h in finals.items():
        h_ref[s, p] = h
        hout_ref[0, s, p, 0] = h[:HEAD_DIM, :HEAD_DIM]
        hout_ref[0, s, p, 1] = h[HEAD_DIM:, HEAD_DIM:]


def _scan_call(vecs, h0, *, steps, n_chunks, grid, row_block, row_map):
    nst = h0.shape[1]
    rb = pl.BlockSpec(row_block + (D_A,), row_map)
    hb = pl.BlockSpec((1, nst, PAIRS, 2, HEAD_DIM, HEAD_DIM), lambda b, t: (b, 0, 0, 0, 0, 0))
    return pl.pallas_call(
        functools.partial(_scan_kernel, steps=steps, n_chunks=n_chunks),
        grid=grid, in_specs=[rb] * 6 + [hb], out_specs=[rb, hb],
        out_shape=[jax.ShapeDtypeStruct(vecs[0].shape, F32), jax.ShapeDtypeStruct(h0.shape, F32)],
        scratch_shapes=[pltpu.VMEM((nst, PAIRS, LANES, LANES), F32)],
        compiler_params=pltpu.CompilerParams(dimension_semantics=("arbitrary", "arbitrary"),
                                             vmem_limit_bytes=VMEM_LIMIT),
        name="scan",
    )(*vecs, h0)


def _post_kernel(y_ref, r_ref, k_ref, v_ref, g_ref, ga_ref, gbb_ref, x_ref, ones_ref, rk_ref, lng_ref,
                 lnb_ref, waup_ref, wo_ref, n2_ref, w1_ref, w2_ref, fn_ref, o_ref, *, final):
    ones = ones_ref[...]
    y = y_ref[...]
    mu = _seg_sum(y, ones) * (1.0 / HEAD_DIM)
    d = y - mu
    var = _seg_sum(d * d, ones) * (1.0 / HEAD_DIM)
    yn = d * lax.rsqrt(var + GN_EPS) * lng_ref[...] + lnb_ref[...]
    bonus = _seg_sum(r_ref[...] * k_ref[...] * rk_ref[...], ones) * v_ref[...]
    a_out = _dot((yn + bonus) * g_ref[...], waup_ref[...])
    merged = ga_ref[...] * a_out + gbb_ref[...]
    x1 = x_ref[...] + _dot(merged, wo_ref[...])
    f = jnp.maximum(_dot(_rms(x1, n2_ref[...]), w1_ref[...]), 0.0)
    x2 = x1 + _dot(f * f, w2_ref[...])
    o_ref[...] = _rms(x2, fn_ref[...]) if final else x2


def _post_call(rows_in, wts, *, final, tm):
    n = rows_in[0].shape[0]
    rb = lambda a: pl.BlockSpec((tm, a.shape[-1]), lambda i: (i, 0))
    return pl.pallas_call(
        functools.partial(_post_kernel, final=final),
        grid=(n // tm,),
        in_specs=[rb(a) for a in rows_in] + [_const_spec(w.shape) for w in wts],
        out_specs=pl.BlockSpec((tm, D_MODEL), lambda i: (i, 0)),
        out_shape=jax.ShapeDtypeStruct((n, D_MODEL), F32),
        compiler_params=pltpu.CompilerParams(dimension_semantics=("arbitrary",),
                                             vmem_limit_bytes=VMEM_LIMIT),
        name="post",
    )(*rows_in, *wts)


def _layer_weights(l, w_in, mu_shift, decay0, w_decay2, a0, w_a2, w_g2, k_k, k_a, r_k, ln_x_g, ln_x_b,
                   w_a_up, w_pool, pool_scale, w_b_up, w_o, norm1_g, norm2_g, w_ff1, w_ff2, ones_bd):
    row = lambda a: a[l].reshape(1, -1)
    zpad = jnp.zeros((LORA_WA // 2, D_A), F32)
    prep = [row(norm1_g), w_in[l].astype(BF16), row(mu_shift), row(decay0),
            jnp.concatenate([w_decay2[l], zpad], axis=0).astype(BF16), row(a0),
            jnp.concatenate([zpad, w_a2[l]], axis=0).astype(BF16), w_g2[l].astype(BF16),
            row(k_k), row(k_a), ones_bd, w_pool[l].astype(BF16), row(pool_scale), w_b_up[l].astype(BF16)]
    post = [ones_bd, row(r_k), row(ln_x_g), row(ln_x_b), w_a_up[l].astype(BF16), w_o[l].astype(BF16),
            row(norm2_g), w_ff1[l].astype(BF16), w_ff2[l].astype(BF16)]
    return prep, post


def kernel(x_prompt, x_sample, state_shift, state_pool, state_wkv, norm1_g, w_in, mu_shift, decay0,
           w_decay2, a0, w_a2, w_g2, k_k, k_a, r_k, ln_x_g, ln_x_b, w_a_up, w_pool, pool_scale, w_b_up,
           w_o, norm2_g, w_ff1, w_ff2, final_norm_g):
    depth = w_in.shape[0]
    bp, tp, _ = x_prompt.shape
    bs, ts, _ = x_sample.shape
    past_len = 16384
    tt_p = 256
    nb_s = 32
    ts_p = 256
    seg = jnp.arange(MXU_DIM) // HEAD_DIM
    ones_bd = (seg[:, None] == seg[None, :]).astype(BF16)
    fn = final_norm_g.reshape(1, -1)

    xp = x_prompt
    xs = x_sample.reshape(bs * ts, D_MODEL)
    outs = {k: [] for k in ("p_shift", "p_pool", "p_wkv", "s_shift", "s_pool", "s_wkv")}
    for l in range(depth):
        prep_w, post_w = _layer_weights(
            l, w_in, mu_shift, decay0, w_decay2, a0, w_a2, w_g2, k_k, k_a, r_k, ln_x_g, ln_x_b, w_a_up,
            w_pool, pool_scale, w_b_up, w_o, norm1_g, norm2_g, w_ff1, w_ff2, ones_bd)
        final = l == depth - 1
        post_w = post_w + [fn]

        res = _prep_call(
            xp, jnp.zeros((bp, SUBLANES, D_SHIFT), F32), jnp.zeros((bp, POOL_CARRY, D_POOL), F32), prep_w,
            nb=1, tt=tt_p, pos0=0, grid=(bp, tp // tt_p), row_block=(1, tt_p),
            row_map=lambda b, t: (b, t, 0), st_map=lambda b, t: (b, 0, 0))
        r, lw, k, v, a, b, g, ga, gbb, sh, pool = res
        outs["p_shift"].append(sh[:, SUBLANES - 1])
        outs["p_pool"].append(pool[:, 1:])
        y, hfin = _scan_call(
            (r, lw, k, v, a, b), jnp.zeros((bp, 1, PAIRS, 2, HEAD_DIM, HEAD_DIM), F32), steps=CHUNK,
            n_chunks=ts_p // CHUNK, grid=(bp, tp // ts_p), row_block=(1, ts_p),
            row_map=lambda b, t: (b, t, 0))
        outs["p_wkv"].append(hfin.reshape(bp, N_HEADS, HEAD_DIM, HEAD_DIM))
        flat = lambda z: z.reshape(bp * tp, z.shape[-1])
        xp = _post_call([flat(z) for z in (y, r, k, v, g, ga, gbb, xp)], post_w, final=final,
                        tm=256).reshape(bp, tp, D_MODEL)

        xs_tm = jnp.swapaxes(xs.reshape(bs, ts, D_MODEL), 0, 1)
        pool_tm = jnp.swapaxes(jnp.pad(state_pool[l], ((0, 0), (1, 0), (0, 0))), 0, 1)
        res = _prep_call(
            xs_tm, state_shift[l][None], pool_tm, prep_w,
            nb=nb_s, tt=ts, pos0=past_len, grid=(bs // nb_s, 1), row_block=(ts, nb_s),
            row_map=lambda b, t: (0, b, 0), st_map=lambda b, t: (0, b, 0))
        sm = lambda z: jnp.swapaxes(z, 0, 1).reshape(bs * ts, z.shape[-1])
        r, lw, k, v, a, b, g, ga, gbb = (sm(z) for z in res[:9])
        outs["s_shift"].append(res[9][0])
        outs["s_pool"].append(jnp.swapaxes(res[10], 0, 1)[:, 1:])
        nseq = CHUNK // ts
        nc_s = 2
        h0 = state_wkv[l].reshape(bs // (nseq * nc_s), nseq * nc_s, PAIRS, 2, HEAD_DIM, HEAD_DIM)
        y, hfin = _scan_call(
            (r, lw, k, v, a, b), h0, steps=ts, n_chunks=nc_s, grid=(bs // (nseq * nc_s), 1),
            row_block=(nc_s * CHUNK,), row_map=lambda b, t: (b, 0))
        outs["s_wkv"].append(hfin.reshape(bs, N_HEADS, HEAD_DIM, HEAD_DIM))
        xs = _post_call([y, r, k, v, g, ga, gbb, xs], post_w, final=final, tm=256)

    st = lambda key: jnp.stack(outs[key])
    return (xp, xs.reshape(bs, ts, D_MODEL), st("p_shift"), st("p_pool"), st("p_wkv"),
            st("s_shift"), st("s_pool"), st("s_wkv"))
```

```python
import functools

import jax
import jax.numpy as jnp
from jax import lax
from jax.experimental import pallas as pl
from jax.experimental.pallas import tpu as pltpu

F32 = jnp.float32
BF16 = jnp.bfloat16

D_MODEL = 1024
HEAD_DIM = 64
D_A = 512
N_HEADS = 8
LORA_WA = 128
LORA_G = 128
D_SHIFT = 3 * D_A + LORA_WA + LORA_G
D_POOL = 512
POOL_WINDOWS = (2, 4, 8, 16)
POOL_GD = 128
POOL_CARRY = 16
D_FF = 4096
RMS_EPS = 1e-6
GN_EPS = HEAD_DIM * 1e-5
L2_EPS = 1e-12

LANES = 128
SUBLANES = 8
MXU_DIM = 256
CHUNK = 64
PAIRS = D_A // LANES
VMEM_LIMIT = 56 * 1024 * 1024


def _dot(a, b):
    return jnp.dot(a.astype(BF16), b.astype(BF16), preferred_element_type=F32)


def _dot_nt(a, b):
    return lax.dot_general(a.astype(BF16), b.astype(BF16), (((1,), (1,)), ((), ())),
                           preferred_element_type=F32)


def _dot_tn(a, b):
    return lax.dot_general(a.astype(BF16), b.astype(BF16), (((0,), (0,)), ((), ())),
                           preferred_element_type=F32)


def _split3(x):
    hi = x.astype(BF16)
    r1 = x - hi.astype(F32)
    mid = r1.astype(BF16)
    lo = (r1 - mid.astype(F32)).astype(BF16)
    return hi, mid, lo


def _dot_sel(sel, x):
    hi, mid, lo = _split3(x)
    d = lambda p: jnp.dot(sel, p, preferred_element_type=F32)
    return d(hi) + d(mid) + d(lo)


def _seg_sum(x, ones_bd):
    hi = x.astype(BF16)
    lo = (x - hi.astype(F32)).astype(BF16)
    w = ones_bd.shape[0]
    d = lambda p: jnp.concatenate(
        [jnp.dot(p[:, i:i + w], ones_bd, preferred_element_type=F32) for i in range(0, x.shape[1], w)],
        axis=1)
    return d(hi) + d(lo)


def _div_pow2(x, d):
    assert d & (d - 1) == 0
    return jnp.right_shift(x, d.bit_length() - 1)


def _rms(x, g):
    return x * lax.rsqrt(jnp.mean(x * x, axis=-1, keepdims=True) + RMS_EPS) * g


def _sigmoid(x):
    return 1.0 / (1.0 + jnp.exp(-x))


def _softplus(x):
    return jnp.maximum(x, 0.0) + jnp.log(1.0 + jnp.exp(-jnp.abs(x)))


def _prep_kernel(x_ref, shift_ref, pool_ref, n1_ref, win_ref, mu_ref, dec0_ref, wdec_ref, a0_ref,
                 wa_ref, wg_ref, kk_ref, ka_ref, ones_ref, wpool_ref, pscale_ref, wbup_ref,
                 r_ref, lw_ref, k_ref, v_ref, a_ref, b_ref, g_ref, ga_ref, gbb_ref,
                 shift_out_ref, pool_out_ref, cz_ref, cu_ref, *, nb, tt, pos0):
    tm = tt * nb
    cz = cz_ref.shape[0]
    cu = cu_ref.shape[0]
    t = pl.program_id(1)
    hist = pool_ref.shape[1] if nb > 1 else POOL_CARRY

    def put(ref, val):
        if nb == 1:
            ref[...] = val.reshape(ref.shape)
        else:
            for i in range(tt):
                ref[:, i, :] = val[i * nb:(i + 1) * nb]

    @pl.when(t == 0)
    def _():
        cz_ref[...] = shift_ref[...].reshape(cz_ref.shape)
        if nb == 1:
            cu_ref[...] = pool_ref[...].reshape(cu_ref.shape)
        else:
            cu_ref[...] = jnp.concatenate(
                [jnp.zeros(((POOL_CARRY - hist) * nb, D_POOL), F32)]
                + [pool_ref[:, i, :] for i in range(hist)], axis=0)

    if nb == 1:
        x = x_ref[...].reshape(tm, D_MODEL)
    else:
        x = jnp.concatenate([x_ref[:, i, :] for i in range(tt)], axis=0)
    hb = _rms(x, n1_ref[...]).astype(BF16)

    z_rw = jnp.dot(hb, win_ref[:, 0:D_SHIFT], preferred_element_type=F32)
    c0 = D_SHIFT + D_POOL
    z_ga = jnp.dot(hb, win_ref[:, c0:c0 + D_MODEL], preferred_element_type=F32)
    z_gb = jnp.dot(hb, win_ref[:, c0 + D_MODEL:c0 + 2 * D_MODEL], preferred_element_type=F32)
    ext = jnp.concatenate([cz_ref[...], z_rw], axis=0)
    z_prev = pltpu.roll(ext, nb, 0)[cz:]
    zs = z_rw + (z_prev - z_rw) * mu_ref[...]
    new_cz = z_rw[tm - cz:]
    cz_ref[...] = new_cz
    shift_out_ref[...] = new_cz.reshape(shift_out_ref.shape)

    r = zs[:, 0:D_A]
    k = zs[:, D_A:2 * D_A]
    v = zs[:, 2 * D_A:3 * D_A]
    lwa = zs[:, 3 * D_A:3 * D_A + LORA_WA]
    lg = zs[:, 3 * D_A + LORA_WA:D_SHIFT]
    w_log = -_softplus(-(dec0_ref[...] + _dot(jnp.tanh(lwa), wdec_ref[...]))) - 0.5
    a_in = _sigmoid(a0_ref[...] + _dot(lwa, wa_ref[...]))
    kk = k * kk_ref[...]
    kk = kk / jnp.maximum(jnp.sqrt(_seg_sum(kk * kk, ones_ref[...])), L2_EPS)
    put(r_ref, r)
    put(lw_ref, -jnp.exp(w_log))
    put(k_ref, k * (1.0 + (a_in - 1.0) * ka_ref[...]))
    put(v_ref, v)
    put(a_ref, -kk)
    put(b_ref, kk * a_in)
    put(g_ref, _dot(_sigmoid(lg), wg_ref[...]))

    u = jnp.dot(hb, win_ref[:, D_SHIFT:D_SHIFT + D_POOL], preferred_element_type=F32)
    extu = jnp.concatenate([cu_ref[...], u], axis=0)
    new_cu = extu[tm:]
    cu_ref[...] = new_cu
    if nb == 1:
        pool_out_ref[...] = new_cu.reshape(pool_out_ref.shape)
    else:
        for i in range(hist):
            pool_out_ref[:, i, :] = new_cu[(POOL_CARRY - hist + i) * nb:(POOL_CARRY - hist + i + 1) * nb]
    row = lax.broadcasted_iota(jnp.int32, (tm, 1), 0)
    step = _div_pow2(row, nb)
    pos = pos0 + t * tt + step
    parts = []
    for gi, win in enumerate(POOL_WINDOWS):
        sl = slice(gi * POOL_GD, (gi + 1) * POOL_GD)
        s = extu[:, sl]
        span = 1
        while span < win:
            s = s + pltpu.roll(s, span * nb, 0)
            span *= 2
        cnt = jnp.minimum(pos + 1, win).astype(F32)
        p = s[cu:] / cnt - u[:, sl]
        parts.append(_dot(p, wpool_ref[gi]))
    pooled = jnp.concatenate(parts, axis=-1) * pscale_ref[...]
    b_out = _dot(pooled, wbup_ref[...])

    put(ga_ref, _sigmoid(z_ga))
    put(gbb_ref, _sigmoid(z_gb) * b_out)


def _const_spec(shape):
    nd = len(shape)
    return pl.BlockSpec(shape, lambda *_: (0,) * nd, pipeline_mode=pl.Buffered(1))


def _prep_call(x, shift_in, pool_in, wts, *, nb, tt, pos0, grid, row_block, row_map, st_map):
    lead = x.shape[:-1]
    outs_rows = lambda w: jax.ShapeDtypeStruct(lead + (w,), F32)
    rb = lambda w: pl.BlockSpec(row_block + (w,), row_map)
    cz = max(SUBLANES, nb)
    cu = POOL_CARRY * nb
    shift_block = (1, cz, D_SHIFT)
    pool_block = (1, POOL_CARRY, D_POOL) if nb == 1 else (nb,) + pool_in.shape[1:]
    shift_spec = pl.BlockSpec(shift_block, st_map)
    pool_spec = pl.BlockSpec(pool_block, lambda b, t: (b, 0, 0))
    in_specs = [rb(D_MODEL), shift_spec, pool_spec] + [_const_spec(w.shape) for w in wts]
    out_shape = [outs_rows(D_A)] * 7 + [outs_rows(D_MODEL)] * 2 + [
        jax.ShapeDtypeStruct(shift_in.shape, F32), jax.ShapeDtypeStruct(pool_in.shape, F32)]
    out_specs = [rb(D_A)] * 7 + [rb(D_MODEL)] * 2 + [shift_spec, pool_spec]
    return pl.pallas_call(
        functools.partial(_prep_kernel, nb=nb, tt=tt, pos0=pos0),
        grid=grid, in_specs=in_specs, out_specs=out_specs, out_shape=out_shape,
        scratch_shapes=[pltpu.VMEM((cz, D_SHIFT), F32), pltpu.VMEM((cu, D_POOL), F32)],
        compiler_params=pltpu.CompilerParams(dimension_semantics=("arbitrary", "arbitrary"),
                                             vmem_limit_bytes=VMEM_LIMIT),
        name="prep",
    )(x, shift_in, pool_in, *wts)


def _scan_masks(steps):
    assert CHUNK == HEAD_DIM
    ri = lax.broadcasted_iota(jnp.int32, (CHUNK, LANES), 0)
    ci = jnp.bitwise_and(lax.broadcasted_iota(jnp.int32, (CHUNK, LANES), 1), CHUNK - 1)
    blk = lambda s: _div_pow2(ri, s) == _div_pow2(ci, s)
    same_seq = blk(steps)
    strict = same_seq & (ri > ci)
    incl = same_seq & (ri >= ci)
    levels = []
    size = SUBLANES
    while size < steps:
        size *= 2
        levels.append(strict & blk(size) & ~blk(size // 2))
    base = strict & blk(SUBLANES)
    eye = ri == ci
    return same_seq, strict, incl, base, levels, eye


def _bdot(a, b):
    return lax.dot_general(a.astype(BF16), b.astype(BF16), (((2,), (1,)), ((0,), (0,))),
                           preferred_element_type=F32)


def _bdot_nt(a, b):
    return lax.dot_general(a.astype(BF16), b.astype(BF16), (((2,), (2,)), ((0,), (0,))),
                           preferred_element_type=F32)


def _bdot_tn(a, b):
    return lax.dot_general(a.astype(BF16), b.astype(BF16), (((1,), (1,)), ((0,), (0,))),
                           preferred_element_type=F32)


def _head_stack(x):
    first = lax.broadcasted_iota(jnp.int32, (1, 1, LANES), 2) < HEAD_DIM
    return jnp.concatenate([jnp.where(first, x, 0.0), jnp.where(first, 0.0, x)], axis=1)


def _hdot(a, b):
    return _bdot(a, _head_stack(b))


def _tri_inverse(a_strict, base, levels, eye):
    ad = jnp.where(base, a_strict, 0.0)
    p2 = _hdot(ad, ad)
    t = jnp.where(eye, 1.0, ad)
    t = t + _hdot(t, p2)
    t = t + _hdot(t, _hdot(p2, p2))
    for lvl in levels:
        t = t + _hdot(t, _hdot(jnp.where(lvl, a_strict, 0.0), t))
    return t


def _scan_kernel(r_ref, lw_ref, k_ref, v_ref, a_ref, b_ref, h0_ref, y_ref, hout_ref, h_ref,
                 *, steps, n_chunks):
    nseq = CHUNK // steps
    nst = h_ref.shape[0]
    t = pl.program_id(1)

    @pl.when(t == 0)
    def _():
        zero = jnp.zeros((HEAD_DIM, HEAD_DIM), F32)
        for s in range(nst):
            for p in range(PAIRS):
                top = jnp.concatenate([h0_ref[0, s, p, 0], zero], axis=1)
                bot = jnp.concatenate([zero, h0_ref[0, s, p, 1]], axis=1)
                h_ref[s, p] = jnp.concatenate([top, bot], axis=0)

    same_seq, strict, incl, base, levels, eye = _scan_masks(steps)
    sel = jnp.concatenate([incl[:, :CHUNK], same_seq[:, :CHUNK]], axis=0).astype(BF16)
    hi = lax.broadcasted_iota(jnp.int32, (LANES, LANES), 0)
    hj = lax.broadcasted_iota(jnp.int32, (LANES, LANES), 1)
    same_head = _div_pow2(hi, HEAD_DIM) == _div_pow2(hj, HEAD_DIM)
    rows = n_chunks * CHUNK

    ld = lambda ref: ref[...].reshape(rows, D_A)
    r, lw, k, v, a, b = (ld(x) for x in (r_ref, lw_ref, k_ref, v_ref, a_ref, b_ref))
    cc = [_dot_sel(sel, lw[c * CHUNK:(c + 1) * CHUNK]) for c in range(n_chunks)]
    cs = jnp.concatenate([x[:CHUNK] for x in cc], axis=0)
    cl = jnp.concatenate([x[CHUNK:] for x in cc], axis=0)
    e_cs = jnp.exp(cs)
    e_ncs = jnp.exp(-cs)
    e_cl = jnp.exp(cl - cs)
    gam = jnp.exp(cl)

    def inst(x):
        x3 = x.reshape(n_chunks, CHUNK, D_A)
        return jnp.concatenate([x3[:, :, p * LANES:(p + 1) * LANES] for p in range(PAIRS)], axis=0)

    at = inst(a * jnp.exp(cs - lw))
    rt = inst(r * e_cs)
    bh = inst(b * e_cl)
    kh = inst(k * e_cl)
    vv = inst(v)
    aa = _bdot_nt(jnp.concatenate([at, rt], axis=1),
                  jnp.concatenate([_head_stack(inst(b * e_ncs)), _head_stack(inst(k * e_ncs))], axis=1))
    a_ab = jnp.where(strict, aa[:, :CHUNK, :LANES], 0.0)
    a_ak = jnp.where(strict, aa[:, :CHUNK, LANES:], 0.0)
    a_rb = jnp.where(incl, aa[:, CHUNK:, :LANES], 0.0)
    a_rk = jnp.where(incl, aa[:, CHUNK:, LANES:], 0.0)
    tinv = _tri_inverse(a_ab, base, levels, eye)
    akv = _hdot(jnp.concatenate([a_ak, a_rk], axis=1), vv)
    tx = _bdot(tinv, jnp.concatenate([_head_stack(at), _head_stack(akv[:, :CHUNK])], axis=2))
    ah = tx[:, :, :LANES]
    w1 = tx[:, :, LANES:]
    rx = _bdot(a_rb, jnp.concatenate([_head_stack(ah), _head_stack(w1)], axis=2))
    rh = rt + rx[:, :, :LANES]
    y1 = rx[:, :, LANES:] + akv[:, CHUNK:]

    ng = PAIRS * n_chunks
    bf16_rows = 2 * SUBLANES

    def seqsplit(x):
        parts = [x[:, s * steps:(s + 1) * steps] for s in range(nseq)]
        if steps < bf16_rows:
            pad = jnp.zeros((x.shape[0], bf16_rows - steps, x.shape[2]), F32)
            parts = [jnp.concatenate([q, pad], axis=1) for q in parts]
        return jnp.concatenate(parts, axis=0)

    bhs = seqsplit(bh)
    mp = jnp.where(same_head, _bdot_tn(seqsplit(ah), bhs), 0.0)
    np_ = jnp.where(same_head, _bdot_tn(jnp.concatenate([seqsplit(w1), seqsplit(vv)], axis=1),
                                        jnp.concatenate([bhs, seqsplit(kh)], axis=1)), 0.0)
    gam_i = inst(gam)
    gam_m = jnp.concatenate([gam_i[:, s * steps:s * steps + 1] for s in range(nseq)], axis=0)
    order = [(s, p, c) for s in range(nseq) for p in range(PAIRS) for c in range(n_chunks)]
    if nst == nseq:
        take = lambda x, c: jnp.concatenate(
            [x[s * ng + p * n_chunks + c][None] for s in range(nseq) for p in range(PAIRS)], axis=0)
        cur = jnp.concatenate([h_ref[s, p][None] for s in range(nseq) for p in range(PAIRS)], axis=0)
        starts = []
        for c in range(n_chunks):
            starts.append(cur)
            cur = cur * take(gam_m, c) + _bdot(cur, take(mp, c)) + take(np_, c)
        s_start = jnp.concatenate(
            [starts[c][s * PAIRS + p][None] for (s, p, c) in order], axis=0)
        finals = {(s, p): cur[s * PAIRS + p] for s in range(nseq) for p in range(PAIRS)}
    else:
        s_start = jnp.concatenate([h_ref[c * nseq + s, p][None] for (s, p, c) in order], axis=0)
        s_end = s_start * gam_m + _bdot(s_start, mp) + np_
        finals = {(c * nseq + s, p): s_end[i] for i, (s, p, c) in enumerate(order)}
    y = (_bdot_nt(seqsplit(rh), s_start) + seqsplit(y1))[:, :steps]
    for i, (s, p, c) in enumerate(order):
        r0 = c * CHUNK + s * steps
        if len(y_ref.shape) == 3:
            y_ref[0, r0:r0 + steps, p * LANES:(p + 1) * LANES] = y[i]
        else:
            y_ref[r0:r0 + steps, p * LANES:(p + 1) * LANES] = y[i]
    for (s, p), h in finals.items():
        h_ref[s, p] = h
        hout_ref[0, s, p, 0] = h[:HEAD_DIM, :HEAD_DIM]
        hout_ref[0, s, p, 1] = h[HEAD_DIM:, HEAD_DIM:]


def _scan_call(vecs, h0, *, steps, n_chunks, grid, row_block, row_map):
    nst = h0.shape[1]
    rb = pl.BlockSpec(row_block + (D_A,), row_map)
    hb = pl.BlockSpec((1, nst, PAIRS, 2, HEAD_DIM, HEAD_DIM), lambda b, t: (b, 0, 0, 0, 0, 0))
    return pl.pallas_call(
        functools.partial(_scan_kernel, steps=steps, n_chunks=n_chunks),
        grid=grid, in_specs=[rb] * 6 + [hb], out_specs=[rb, hb],
        out_shape=[jax.ShapeDtypeStruct(vecs[0].shape, F32), jax.ShapeDtypeStruct(h0.shape, F32)],
        scratch_shapes=[pltpu.VMEM((nst, PAIRS, LANES, LANES), F32)],
        compiler_params=pltpu.CompilerParams(dimension_semantics=("arbitrary", "arbitrary"),
                                             vmem_limit_bytes=VMEM_LIMIT),
        name="scan",
    )(*vecs, h0)


def _post_kernel(y_ref, r_ref, k_ref, v_ref, g_ref, ga_ref, gbb_ref, x_ref, ones_ref, rk_ref, lng_ref,
                 lnb_ref, waup_ref, wo_ref, n2_ref, w1_ref, w2_ref, fn_ref, o_ref, *, final):
    ones = ones_ref[...]
    y = y_ref[...]
    mu = _seg_sum(y, ones) * (1.0 / HEAD_DIM)
    d = y - mu
    var = _seg_sum(d * d, ones) * (1.0 / HEAD_DIM)
    yn = d * lax.rsqrt(var + GN_EPS) * lng_ref[...] + lnb_ref[...]
    bonus = _seg_sum(r_ref[...] * k_ref[...] * rk_ref[...], ones) * v_ref[...]
    a_out = _dot((yn + bonus) * g_ref[...], waup_ref[...])
    merged = ga_ref[...] * a_out + gbb_ref[...]
    x1 = x_ref[...] + _dot(merged, wo_ref[...])
    f = jnp.maximum(_dot(_rms(x1, n2_ref[...]), w1_ref[...]), 0.0)
    x2 = x1 + _dot(f * f, w2_ref[...])
    o_ref[...] = _rms(x2, fn_ref[...]) if final else x2


def _post_call(rows_in, wts, *, final, tm):
    n = rows_in[0].shape[0]
    rb = lambda a: pl.BlockSpec((tm, a.shape[-1]), lambda i: (i, 0))
    return pl.pallas_call(
        functools.partial(_post_kernel, final=final),
        grid=(n // tm,),
        in_specs=[rb(a) for a in rows_in] + [_const_spec(w.shape) for w in wts],
        out_specs=pl.BlockSpec((tm, D_MODEL), lambda i: (i, 0)),
        out_shape=jax.ShapeDtypeStruct((n, D_MODEL), F32),
        compiler_params=pltpu.CompilerParams(dimension_semantics=("arbitrary",),
                                             vmem_limit_bytes=VMEM_LIMIT),
        name="post",
    )(*rows_in, *wts)


def _layer_weights(l, w_in, mu_shift, decay0, w_decay2, a0, w_a2, w_g2, k_k, k_a, r_k, ln_x_g, ln_x_b,
                   w_a_up, w_pool, pool_scale, w_b_up, w_o, norm1_g, norm2_g, w_ff1, w_ff2, ones_bd):
    row = lambda a: a[l].reshape(1, -1)
    zpad = jnp.zeros((LORA_WA // 2, D_A), F32)
    prep = [row(norm1_g), w_in[l].astype(BF16), row(mu_shift), row(decay0),
            jnp.concatenate([w_decay2[l], zpad], axis=0).astype(BF16), row(a0),
            jnp.concatenate([zpad, w_a2[l]], axis=0).astype(BF16), w_g2[l].astype(BF16),
            row(k_k), row(k_a), ones_bd, w_pool[l].astype(BF16), row(pool_scale), w_b_up[l].astype(BF16)]
    post = [ones_bd, row(r_k), row(ln_x_g), row(ln_x_b), w_a_up[l].astype(BF16), w_o[l].astype(BF16),
            row(norm2_g), w_ff1[l].astype(BF16), w_ff2[l].astype(BF16)]
    return prep, post


def kernel(x_prompt, x_sample, state_shift, state_pool, state_wkv, norm1_g, w_in, mu_shift, decay0,
           w_decay2, a0, w_a2, w_g2, k_k, k_a, r_k, ln_x_g, ln_x_b, w_a_up, w_pool, pool_scale, w_b_up,
           w_o, norm2_g, w_ff1, w_ff2, final_norm_g):
    depth = w_in.shape[0]
    bp, tp, _ = x_prompt.shape
    bs, ts, _ = x_sample.shape
    past_len = 16384
    tt_p = 256
    nb_s = 32
    ts_p = 512
    seg = jnp.arange(MXU_DIM) // HEAD_DIM
    ones_bd = (seg[:, None] == seg[None, :]).astype(BF16)
    fn = final_norm_g.reshape(1, -1)

    xp = x_prompt
    xs = x_sample.reshape(bs * ts, D_MODEL)
    outs = {k: [] for k in ("p_shift", "p_pool", "p_wkv", "s_shift", "s_pool", "s_wkv")}
    for l in range(depth):
        prep_w, post_w = _layer_weights(
            l, w_in, mu_shift, decay0, w_decay2, a0, w_a2, w_g2, k_k, k_a, r_k, ln_x_g, ln_x_b, w_a_up,
            w_pool, pool_scale, w_b_up, w_o, norm1_g, norm2_g, w_ff1, w_ff2, ones_bd)
        final = l == depth - 1
        post_w = post_w + [fn]

        res = _prep_call(
            xp, jnp.zeros((bp, SUBLANES, D_SHIFT), F32), jnp.zeros((bp, POOL_CARRY, D_POOL), F32), prep_w,
            nb=1, tt=tt_p, pos0=0, grid=(bp, tp // tt_p), row_block=(1, tt_p),
            row_map=lambda b, t: (b, t, 0), st_map=lambda b, t: (b, 0, 0))
        r, lw, k, v, a, b, g, ga, gbb, sh, pool = res
        outs["p_shift"].append(sh[:, SUBLANES - 1])
        outs["p_pool"].append(pool[:, 1:])
        y, hfin = _scan_call(
            (r, lw, k, v, a, b), jnp.zeros((bp, 1, PAIRS, 2, HEAD_DIM, HEAD_DIM), F32), steps=CHUNK,
            n_chunks=ts_p // CHUNK, grid=(bp, tp // ts_p), row_block=(1, ts_p),
            row_map=lambda b, t: (b, t, 0))
        outs["p_wkv"].append(hfin.reshape(bp, N_HEADS, HEAD_DIM, HEAD_DIM))
        flat = lambda z: z.reshape(bp * tp, z.shape[-1])
        xp = _post_call([flat(z) for z in (y, r, k, v, g, ga, gbb, xp)], post_w, final=final,
                        tm=256).reshape(bp, tp, D_MODEL)

        res = _prep_call(
            xs.reshape(bs, ts, D_MODEL), state_shift[l][None], state_pool[l], prep_w,
            nb=nb_s, tt=ts, pos0=past_len, grid=(bs // nb_s, 1), row_block=(nb_s, ts),
            row_map=lambda b, t: (b, 0, 0), st_map=lambda b, t: (0, b, 0))
        r, lw, k, v, a, b, g, ga, gbb = (z.reshape(bs * ts, z.shape[-1]) for z in res[:9])
        outs["s_shift"].append(res[9][0])
        outs["s_pool"].append(res[10])
        nseq = CHUNK // ts
        nc_s = 2
        h0 = state_wkv[l].reshape(bs // (nseq * nc_s), nseq * nc_s, PAIRS, 2, HEAD_DIM, HEAD_DIM)
        y, hfin = _scan_call(
            (r, lw, k, v, a, b), h0, steps=ts, n_chunks=nc_s, grid=(bs // (nseq * nc_s), 1),
            row_block=(nc_s * CHUNK,), row_map=lambda b, t: (b, 0))
        outs["s_wkv"].append(hfin.reshape(bs, N_HEADS, HEAD_DIM, HEAD_DIM))
        xs = _post_call([y, r, k, v, g, ga, gbb, xs], post_w, final=final, tm=256)

    st = lambda key: jnp.stack(outs[key])
    return (xp, xs.reshape(bs, ts, D_MODEL), st("p_shift"), st("p_pool"), st("p_wkv"),
            st("s_shift"), st("s_pool"), st("s_wkv"))
```

```python
import functools

import jax
import jax.numpy as jnp
from jax import lax
from jax.experimental import pallas as pl
from jax.experimental.pallas import tpu as pltpu

F32 = jnp.float32
BF16 = jnp.bfloat16

D_MODEL = 1024
HEAD_DIM = 64
D_A = 512
N_HEADS = 8
LORA_WA = 128
LORA_G = 128
D_SHIFT = 3 * D_A + LORA_WA + LORA_G
D_POOL = 512
POOL_WINDOWS = (2, 4, 8, 16)
POOL_GD = 128
POOL_CARRY = 16
D_FF = 4096
RMS_EPS = 1e-6
GN_EPS = HEAD_DIM * 1e-5
L2_EPS = 1e-12

LANES = 128
SUBLANES = 8
MXU_DIM = 256
CHUNK = 64
PAIRS = D_A // LANES
VMEM_LIMIT = 56 * 1024 * 1024


def _dot(a, b):
    return jnp.dot(a.astype(BF16), b.astype(BF16), preferred_element_type=F32)


def _dot_nt(a, b):
    return lax.dot_general(a.astype(BF16), b.astype(BF16), (((1,), (1,)), ((), ())),
                           preferred_element_type=F32)


def _dot_tn(a, b):
    return lax.dot_general(a.astype(BF16), b.astype(BF16), (((0,), (0,)), ((), ())),
                           preferred_element_type=F32)


def _split3(x):
    hi = x.astype(BF16)
    r1 = x - hi.astype(F32)
    mid = r1.astype(BF16)
    lo = (r1 - mid.astype(F32)).astype(BF16)
    return hi, mid, lo


def _dot_sel(sel, x):
    hi, mid, lo = _split3(x)
    d = lambda p: jnp.dot(sel, p, preferred_element_type=F32)
    return d(hi) + d(mid) + d(lo)


def _seg_sum(x, ones_bd):
    hi = x.astype(BF16)
    lo = (x - hi.astype(F32)).astype(BF16)
    w = ones_bd.shape[0]
    d = lambda p: jnp.concatenate(
        [jnp.dot(p[:, i:i + w], ones_bd, preferred_element_type=F32) for i in range(0, x.shape[1], w)],
        axis=1)
    return d(hi) + d(lo)


def _div_pow2(x, d):
    assert d & (d - 1) == 0
    return jnp.right_shift(x, d.bit_length() - 1)


def _rms(x, g):
    return x * lax.rsqrt(jnp.mean(x * x, axis=-1, keepdims=True) + RMS_EPS) * g


def _sigmoid(x):
    return 1.0 / (1.0 + jnp.exp(-x))


def _softplus(x):
    return jnp.maximum(x, 0.0) + jnp.log(1.0 + jnp.exp(-jnp.abs(x)))


def _prep_kernel(x_ref, shift_ref, pool_ref, *rest, nb, tt, pos0, n_prev):
    rest = list(rest)
    pool_prev_ref = rest.pop(0) if n_prev else None
    (n1_ref, win_ref, mu_ref, dec0_ref, wdec_ref, a0_ref, wa_ref, wg_ref, kk_ref, ka_ref, ones_ref,
     wpool_ref, pscale_ref, wbup_ref,
     r_ref, lw_ref, k_ref, v_ref, a_ref, b_ref, g_ref, ga_ref, gbb_ref,
     shift_out_ref, pool_out_ref, cz_ref, cu_ref) = rest
    if n_prev:
        pool_out_ref[0:n_prev] = pool_prev_ref[...]
    tm = tt * nb
    cz = cz_ref.shape[0]
    cu = cu_ref.shape[0]
    t = pl.program_id(1)
    hist = pool_ref.shape[1] if nb > 1 else POOL_CARRY

    def put(ref, val):
        if nb == 1:
            ref[...] = val.reshape(ref.shape)
        else:
            for i in range(tt):
                ref[:, i, :] = val[i * nb:(i + 1) * nb]

    @pl.when(t == 0)
    def _():
        cz_ref[...] = shift_ref[...].reshape(cz_ref.shape)
        if nb == 1:
            cu_ref[...] = pool_ref[...].reshape(cu_ref.shape)
        else:
            cu_ref[...] = jnp.concatenate(
                [jnp.zeros(((POOL_CARRY - hist) * nb, D_POOL), F32)]
                + [pool_ref[:, i, :] for i in range(hist)], axis=0)

    if nb == 1:
        x = x_ref[...].reshape(tm, D_MODEL)
    else:
        x = jnp.concatenate([x_ref[:, i, :] for i in range(tt)], axis=0)
    hb = _rms(x, n1_ref[...]).astype(BF16)

    c0 = D_SHIFT + D_POOL
    z_rw = jnp.dot(hb, win_ref[:, 0:D_SHIFT], preferred_element_type=F32)
    u = jnp.dot(hb, win_ref[:, D_SHIFT:c0], preferred_element_type=F32)
    z_ga = jnp.dot(hb, win_ref[:, c0:c0 + D_MODEL], preferred_element_type=F32)
    z_gb = jnp.dot(hb, win_ref[:, c0 + D_MODEL:c0 + 2 * D_MODEL], preferred_element_type=F32)

    ext = jnp.concatenate([cz_ref[...], z_rw], axis=0)
    z_prev = pltpu.roll(ext, nb, 0)[cz:]
    zs = z_rw + (z_prev - z_rw) * mu_ref[...]
    new_cz = z_rw[tm - cz:]
    cz_ref[...] = new_cz
    shift_out_ref[...] = new_cz.reshape(shift_out_ref.shape)

    r = zs[:, 0:D_A]
    k = zs[:, D_A:2 * D_A]
    v = zs[:, 2 * D_A:3 * D_A]
    lwa = zs[:, 3 * D_A:3 * D_A + LORA_WA]
    lg = zs[:, 3 * D_A + LORA_WA:D_SHIFT]
    w_log = -_softplus(-(dec0_ref[...] + _dot(jnp.tanh(lwa), wdec_ref[...]))) - 0.5
    a_in = _sigmoid(a0_ref[...] + _dot(lwa, wa_ref[...]))
    kk = k * kk_ref[...]
    kk = kk / jnp.maximum(jnp.sqrt(_seg_sum(kk * kk, ones_ref[...])), L2_EPS)
    put(r_ref, r)
    put(lw_ref, -jnp.exp(w_log))
    put(k_ref, k * (1.0 + (a_in - 1.0) * ka_ref[...]))
    put(v_ref, v)
    put(a_ref, -kk)
    put(b_ref, kk * a_in)
    put(g_ref, _dot(_sigmoid(lg), wg_ref[...]))

    extu =jnp.concatenate([cu_ref[...], u], axis=0)
    new_cu = extu[tm:]
    cu_ref[...] = new_cu
    if nb == 1:
        pool_out_ref[n_prev] = new_cu.reshape(pool_out_ref.shape[1:])
    else:
        for i in range(hist):
            i0 = (POOL_CARRY - hist + i) * nb
            pool_out_ref[n_prev, :, i, :] = new_cu[i0:i0 + nb]
    row = lax.broadcasted_iota(jnp.int32, (tm, 1), 0)
    step = _div_pow2(row, nb)
    pos = pos0 + t * tt + step
    parts = []
    for gi, win in enumerate(POOL_WINDOWS):
        sl = slice(gi * POOL_GD, (gi + 1) * POOL_GD)
        s = extu[:, sl]
        span = 1
        while span < win:
            s = s + pltpu.roll(s, span * nb, 0)
            span *= 2
        cnt = jnp.minimum(pos + 1, win).astype(F32)
        p = s[cu:] / cnt - u[:, sl]
        parts.append(_dot(p, wpool_ref[gi]))
    pooled = jnp.concatenate(parts, axis=-1) * pscale_ref[...]
    b_out = _dot(pooled, wbup_ref[...])

    put(ga_ref, _sigmoid(z_ga))
    put(gbb_ref, _sigmoid(z_gb) * b_out)


def _const_spec(shape):
    nd = len(shape)
    return pl.BlockSpec(shape, lambda *_: (0,) * nd, pipeline_mode=pl.Buffered(1))


def _prep_call(x, shift_in, pool_in, pool_prev, wts, *, nb, tt, pos0, grid, row_block, row_map, st_map):
    lead = x.shape[:-1]
    n_prev = 0 if pool_prev is None else pool_prev.shape[0]
    outs_rows = lambda w: jax.ShapeDtypeStruct(lead + (w,), F32)
    rb = lambda w: pl.BlockSpec(row_block + (w,), row_map)
    cz = max(SUBLANES, nb)
    cu = POOL_CARRY * nb
    shift_block = (1, cz, D_SHIFT)
    pool_block = (1, POOL_CARRY, D_POOL) if nb == 1 else (nb,) + pool_in.shape[1:]
    shift_spec = pl.BlockSpec(shift_block, st_map)
    pool_spec = pl.BlockSpec(pool_block, lambda b, t: (b, 0, 0))
    stacked = lambda n: pl.BlockSpec((n,) + pool_block, lambda b, t: (0, b, 0, 0))
    in_specs = ([rb(D_MODEL), shift_spec, pool_spec] + ([stacked(n_prev)] if n_prev else [])
                + [_const_spec(w.shape) for w in wts])
    out_shape = [outs_rows(D_A)] * 7 + [outs_rows(D_MODEL)] * 2 + [
        jax.ShapeDtypeStruct(shift_in.shape, F32), jax.ShapeDtypeStruct((n_prev + 1,) + pool_in.shape, F32)]
    out_specs = [rb(D_A)] * 7 + [rb(D_MODEL)] * 2 + [shift_spec, stacked(n_prev + 1)]
    return pl.pallas_call(
        functools.partial(_prep_kernel, nb=nb, tt=tt, pos0=pos0, n_prev=n_prev),
        grid=grid, in_specs=in_specs, out_specs=out_specs, out_shape=out_shape,
        scratch_shapes=[pltpu.VMEM((cz, D_SHIFT), F32), pltpu.VMEM((cu, D_POOL), F32)],
        compiler_params=pltpu.CompilerParams(dimension_semantics=("arbitrary", "arbitrary"),
                                             vmem_limit_bytes=VMEM_LIMIT),
        name="prep",
    )(x, shift_in, pool_in, *([pool_prev] if n_prev else []), *wts)


def _scan_masks(steps):
    assert CHUNK == HEAD_DIM
    ri = lax.broadcasted_iota(jnp.int32, (CHUNK, LANES), 0)
    ci = jnp.bitwise_and(lax.broadcasted_iota(jnp.int32, (CHUNK, LANES), 1), CHUNK - 1)
    blk = lambda s: _div_pow2(ri, s) == _div_pow2(ci, s)
    same_seq = blk(steps)
    strict = same_seq & (ri > ci)
    incl = same_seq & (ri >= ci)
    levels = []
    size = SUBLANES
    while size < steps:
        size *= 2
        levels.append(strict & blk(size) & ~blk(size // 2))
    base = strict & blk(SUBLANES)
    eye = ri == ci
    return same_seq, strict, incl, base, levels, eye


def _bdot(a, b):
    return lax.dot_general(a.astype(BF16), b.astype(BF16), (((2,), (1,)), ((0,), (0,))),
                           preferred_element_type=F32)


def _bdot_nt(a, b):
    return lax.dot_general(a.astype(BF16), b.astype(BF16), (((2,), (2,)), ((0,), (0,))),
                           preferred_element_type=F32)


def _bdot_tn(a, b):
    return lax.dot_general(a.astype(BF16), b.astype(BF16), (((1,), (1,)), ((0,), (0,))),
                           preferred_element_type=F32)


def _head_stack(x):
    first = lax.broadcasted_iota(jnp.int32, (1, 1, LANES), 2) < HEAD_DIM
    return jnp.concatenate([jnp.where(first, x, 0.0), jnp.where(first, 0.0, x)], axis=1)


def _hdot(a, b):
    return _bdot(a, _head_stack(b))


def _tri_inverse(a_strict, base, levels, eye):
    ad = jnp.where(base, a_strict, 0.0)
    p2 = _hdot(ad, ad)
    t = jnp.where(eye, 1.0, ad)
    t = t + _hdot(t, p2)
    t = t + _hdot(t, _hdot(p2, p2))
    for lvl in levels:
        t = t + _hdot(t, _hdot(jnp.where(lvl, a_strict, 0.0), t))
    return t


def _scan_kernel(r_ref, lw_ref, k_ref, v_ref, a_ref, b_ref, h0_ref, *rest, steps, n_chunks, n_prev):
    y_ref, hout_ref, h_ref = rest[-3:]
    if n_prev:
        hout_ref[0:n_prev] = rest[0][...]
    nseq = CHUNK // steps
    nst = h_ref.shape[0]
    t = pl.program_id(1)

    @pl.when(t == 0)
    def _():
        zero = jnp.zeros((HEAD_DIM, HEAD_DIM), F32)
        for s in range(nst):
            for p in range(PAIRS):
                top = jnp.concatenate([h0_ref[0, s, p, 0], zero], axis=1)
                bot = jnp.concatenate([zero, h0_ref[0, s, p, 1]], axis=1)
                h_ref[s, p] = jnp.concatenate([top, bot], axis=0)

    same_seq, strict, incl, base, levels, eye = _scan_masks(steps)
    sel = jnp.concatenate([incl[:, :CHUNK], same_seq[:, :CHUNK]], axis=0).astype(BF16)
    hi = lax.broadcasted_iota(jnp.int32, (LANES, LANES), 0)
    hj = lax.broadcasted_iota(jnp.int32, (LANES, LANES), 1)
    same_head = _div_pow2(hi, HEAD_DIM) == _div_pow2(hj, HEAD_DIM)
    rows = n_chunks * CHUNK

    ld = lambda ref: ref[...].reshape(rows, D_A)
    r, lw, k, v, a, b = (ld(x) for x in (r_ref, lw_ref, k_ref, v_ref, a_ref, b_ref))
    cc = [_dot_sel(sel, lw[c * CHUNK:(c + 1) * CHUNK]) for c in range(n_chunks)]
    cs = jnp.concatenate([x[:CHUNK] for x in cc], axis=0)
    cl = jnp.concatenate([x[CHUNK:] for x in cc], axis=0)
    e_cs = jnp.exp(cs)
    e_ncs = jnp.exp(-cs)
    e_cl = jnp.exp(cl - cs)
    gam = jnp.exp(cl)

    def inst(x):
        x3 = x.reshape(n_chunks, CHUNK, D_A)
        return jnp.concatenate([x3[:, :, p * LANES:(p + 1) * LANES] for p in range(PAIRS)], axis=0)

    at = inst(a * jnp.exp(cs - lw))
    rt = inst(r * e_cs)
    bh = inst(b * e_cl)
    kh = inst(k * e_cl)
    vv = inst(v)
    aa = _bdot_nt(jnp.concatenate([at, rt], axis=1),
                  jnp.concatenate([_head_stack(inst(b * e_ncs)), _head_stack(inst(k * e_ncs))], axis=1))
    a_ab = jnp.where(strict, aa[:, :CHUNK, :LANES], 0.0)
    a_ak = jnp.where(strict, aa[:, :CHUNK, LANES:], 0.0)
    a_rb = jnp.where(incl, aa[:, CHUNK:, :LANES], 0.0)
    a_rk = jnp.where(incl, aa[:, CHUNK:, LANES:], 0.0)
    tinv = _tri_inverse(a_ab, base, levels, eye)
    akv = _hdot(jnp.concatenate([a_ak, a_rk], axis=1), vv)
    tx = _bdot(tinv, jnp.concatenate([_head_stack(at), _head_stack(akv[:, :CHUNK])], axis=2))
    ah = tx[:, :, :LANES]
    w1 = tx[:, :, LANES:]
    rx = _bdot(a_rb, jnp.concatenate([_head_stack(ah), _head_stack(w1)], axis=2))
    rh = rt + rx[:, :, :LANES]
    y1 = rx[:, :, LANES:] + akv[:, CHUNK:]

    ng = PAIRS * n_chunks
    bf16_rows = 2 * SUBLANES

    def seqsplit(x):
        parts = [x[:, s * steps:(s + 1) * steps] for s in range(nseq)]
        if steps < bf16_rows:
            pad = jnp.zeros((x.shape[0], bf16_rows - steps, x.shape[2]), F32)
            parts = [jnp.concatenate([q, pad], axis=1) for q in parts]
        return jnp.concatenate(parts, axis=0)

    bhs = seqsplit(bh)
    mp = jnp.where(same_head, _bdot_tn(seqsplit(ah), bhs), 0.0)
    np_ = jnp.where(same_head, _bdot_tn(jnp.concatenate([seqsplit(w1), seqsplit(vv)], axis=1),
                                        jnp.concatenate([bhs, seqsplit(kh)], axis=1)), 0.0)
    gam_i = inst(gam)
    gam_m = jnp.concatenate([gam_i[:, s * steps:s * steps + 1] for s in range(nseq)], axis=0)
    order = [(s, p, c) for s in range(nseq) for p in range(PAIRS) for c in range(n_chunks)]
    if nst == nseq:
        take = lambda x, c: jnp.concatenate(
            [x[s * ng + p * n_chunks + c][None] for s in range(nseq) for p in range(PAIRS)], axis=0)
        cur = jnp.concatenate([h_ref[s, p][None] for s in range(nseq) for p in range(PAIRS)], axis=0)
        starts = []
        for c in range(n_chunks):
            starts.append(cur)
            cur = cur * take(gam_m, c) + _bdot(cur, take(mp, c)) + take(np_, c)
        s_start = jnp.concatenate(
            [starts[c][s * PAIRS + p][None] for (s, p, c) in order], axis=0)
        finals = {(s, p): cur[s * PAIRS + p] for s in range(nseq) for p in range(PAIRS)}
    else:
        s_start = jnp.concatenate([h_ref[c * nseq + s, p][None] for (s, p, c) in order], axis=0)
        s_end = s_start * gam_m + _bdot(s_start, mp) + np_
        finals = {(c * nseq + s, p): s_end[i] for i, (s, p, c) in enumerate(order)}
    y = (_bdot_nt(seqsplit(rh), s_start) + seqsplit(y1))[:, :steps]
    for i, (s, p, c) in enumerate(order):
        r0 = c * CHUNK + s * steps
        if len(y_ref.shape) == 3:
            y_ref[0, r0:r0 + steps, p * LANES:(p + 1) * LANES] = y[i]
        else:
            y_ref[r0:r0 + steps, p * LANES:(p + 1) * LANES] = y[i]
    for (s, p), h in finals.items():
        h_ref[s, p] = h
        hout_ref[n_prev, 0, s, p, 0] = h[:HEAD_DIM, :HEAD_DIM]
        hout_ref[n_prev, 0, s, p, 1] = h[HEAD_DIM:, HEAD_DIM:]


def _scan_call(vecs, h0, prev, *, steps, n_chunks, grid, row_block, row_map):
    nst = h0.shape[1]
    n_prev = 0 if prev is None else prev.shape[0]
    st_block = (1, nst, PAIRS, 2, HEAD_DIM, HEAD_DIM)
    rb = pl.BlockSpec(row_block + (D_A,), row_map)
    hb = pl.BlockSpec(st_block, lambda b, t: (b, 0, 0, 0, 0, 0))
    stacked = lambda n: pl.BlockSpec((n,) + st_block, lambda b, t: (0, b, 0, 0, 0, 0, 0))
    return pl.pallas_call(
        functools.partial(_scan_kernel, steps=steps, n_chunks=n_chunks, n_prev=n_prev),
        grid=grid, in_specs=[rb] * 6 + [hb] + ([stacked(n_prev)] if n_prev else []),
        out_specs=[rb, stacked(n_prev + 1)],
        out_shape=[jax.ShapeDtypeStruct(vecs[0].shape, F32),
                   jax.ShapeDtypeStruct((n_prev + 1,) + h0.shape, F32)],
        scratch_shapes=[pltpu.VMEM((nst, PAIRS, LANES, LANES), F32)],
        compiler_params=pltpu.CompilerParams(dimension_semantics=("arbitrary", "arbitrary"),
                                             vmem_limit_bytes=VMEM_LIMIT),
        name="scan",
    )(*vecs, h0, *([prev] if n_prev else []))


def _post_kernel(y_ref, r_ref, k_ref, v_ref, g_ref, ga_ref, gbb_ref, x_ref, ones_ref, rk_ref, lng_ref,
                 lnb_ref, waup_ref, wo_ref, n2_ref, w1_ref, w2_ref, fn_ref, o_ref, *, final):
    ones = ones_ref[...]
    y = y_ref[...]
    mu = _seg_sum(y, ones) * (1.0 / HEAD_DIM)
    d = y - mu
    var = _seg_sum(d * d, ones) * (1.0 / HEAD_DIM)
    yn = d * lax.rsqrt(var + GN_EPS) * lng_ref[...] + lnb_ref[...]
    bonus = _seg_sum(r_ref[...] * k_ref[...] * rk_ref[...], ones) * v_ref[...]
    a_out = _dot((yn + bonus) * g_ref[...], waup_ref[...])
    merged = ga_ref[...] * a_out + gbb_ref[...]
    x1 = x_ref[...] + _dot(merged, wo_ref[...])
    f = jnp.maximum(_dot(_rms(x1, n2_ref[...]), w1_ref[...]), 0.0)
    x2 = x1 + _dot(f * f, w2_ref[...])
    o_ref[...] = _rms(x2, fn_ref[...]) if final else x2


def _post_call(rows_in, wts, *, final, tm):
    n = rows_in[0].shape[0]
    rb = lambda a: pl.BlockSpec((tm, a.shape[-1]), lambda i: (i, 0))
    return pl.pallas_call(
        functools.partial(_post_kernel, final=final),
        grid=(n // tm,),
        in_specs=[rb(a) for a in rows_in] + [_const_spec(w.shape) for w in wts],
        out_specs=pl.BlockSpec((tm, D_MODEL), lambda i: (i, 0)),
        out_shape=jax.ShapeDtypeStruct((n, D_MODEL), F32),
        compiler_params=pltpu.CompilerParams(dimension_semantics=("arbitrary",),
                                             vmem_limit_bytes=VMEM_LIMIT),
        name="post",
    )(*rows_in, *wts)


def _layer_weights(l, w_in, mu_shift, decay0, w_decay2, a0, w_a2, w_g2, k_k, k_a, r_k, ln_x_g, ln_x_b,
                   w_a_up, w_pool, pool_scale, w_b_up, w_o, norm1_g, norm2_g, w_ff1, w_ff2, ones_bd):
    row = lambda a: a[l].reshape(1, -1)
    zpad = jnp.zeros((LORA_WA // 2, D_A), F32)
    prep = [row(norm1_g), w_in[l].astype(BF16), row(mu_shift), row(decay0),
            jnp.concatenate([w_decay2[l], zpad], axis=0).astype(BF16), row(a0),
            jnp.concatenate([zpad, w_a2[l]], axis=0).astype(BF16), w_g2[l].astype(BF16),
            row(k_k), row(k_a), ones_bd, w_pool[l].astype(BF16), row(pool_scale), w_b_up[l].astype(BF16)]
    post = [ones_bd, row(r_k), row(ln_x_g), row(ln_x_b), w_a_up[l].astype(BF16), w_o[l].astype(BF16),
            row(norm2_g), w_ff1[l].astype(BF16), w_ff2[l].astype(BF16)]
    return prep, post


def kernel(x_prompt, x_sample, state_shift, state_pool, state_wkv, norm1_g, w_in, mu_shift, decay0,
           w_decay2, a0, w_a2, w_g2, k_k, k_a, r_k, ln_x_g, ln_x_b, w_a_up, w_pool, pool_scale, w_b_up,
           w_o, norm2_g, w_ff1, w_ff2, final_norm_g):
    depth = w_in.shape[0]
    bp, tp, _ = x_prompt.shape
    bs, ts, _ = x_sample.shape
    past_len = 16384
    tt_p = 256
    nb_s = 32
    ts_p = 512
    seg = jnp.arange(MXU_DIM) // HEAD_DIM
    ones_bd = (seg[:, None] == seg[None, :]).astype(BF16)
    fn = final_norm_g.reshape(1, -1)

    xp = x_prompt
    xs = x_sample.reshape(bs * ts, D_MODEL)
    outs = {k: [] for k in ("p_shift", "s_shift")}
    p_pool = p_wkv = s_pool = s_wkv = None
    for l in range(depth):
        prep_w, post_w = _layer_weights(
            l, w_in, mu_shift, decay0, w_decay2, a0, w_a2, w_g2, k_k, k_a, r_k, ln_x_g, ln_x_b, w_a_up,
            w_pool, pool_scale, w_b_up, w_o, norm1_g, norm2_g, w_ff1, w_ff2, ones_bd)
        final = l == depth - 1
        post_w = post_w + [fn]

        res = _prep_call(
            xp, jnp.zeros((bp, SUBLANES, D_SHIFT), F32), jnp.zeros((bp, POOL_CARRY, D_POOL), F32), p_pool,
            prep_w, nb=1, tt=tt_p, pos0=0, grid=(bp, tp // tt_p), row_block=(1, tt_p),
            row_map=lambda b, t: (b, t, 0), st_map=lambda b, t: (b, 0, 0))
        r, lw, k, v, a, b, g, ga, gbb, sh, p_pool = res
        outs["p_shift"].append(sh[:, SUBLANES - 1])
        y, p_wkv = _scan_call(
            (r, lw, k, v, a, b), jnp.zeros((bp, 1, PAIRS, 2, HEAD_DIM, HEAD_DIM), F32), p_wkv,
            steps=CHUNK, n_chunks=ts_p // CHUNK, grid=(bp, tp // ts_p), row_block=(1, ts_p),
            row_map=lambda b, t: (b, t, 0))
        flat = lambda z: z.reshape(bp * tp, z.shape[-1])
        xp = _post_call([flat(z) for z in (y, r, k, v, g, ga, gbb, xp)], post_w, final=final,
                        tm=256).reshape(bp, tp, D_MODEL)

        res = _prep_call(
            xs.reshape(bs, ts, D_MODEL), state_shift[l][None], state_pool[l], s_pool, prep_w,
            nb=nb_s, tt=ts, pos0=past_len, grid=(bs // nb_s, 1), row_block=(nb_s, ts),
            row_map=lambda b, t: (b, 0, 0), st_map=lambda b, t: (0, b, 0))
        r, lw, k, v, a, b, g, ga, gbb = (z.reshape(bs * ts, z.shape[-1]) for z in res[:9])
        outs["s_shift"].append(res[9][0])
        s_pool = res[10]
        nseq = CHUNK // ts
        nc_s = 2
        h0 = state_wkv[l].reshape(bs // (nseq * nc_s), nseq * nc_s, PAIRS, 2, HEAD_DIM, HEAD_DIM)
        y, s_wkv = _scan_call(
            (r, lw, k, v, a, b), h0, s_wkv, steps=ts, n_chunks=nc_s, grid=(bs // (nseq * nc_s), 1),
            row_block=(nc_s * CHUNK,), row_map=lambda b, t: (b, 0))
        xs = _post_call([y, r, k, v, g, ga, gbb, xs], post_w, final=final, tm=256)

    st = lambda key: jnp.stack(outs[key])
    wkv = lambda h, n: h.reshape(depth, n, N_HEADS, HEAD_DIM, HEAD_DIM)
    return (xp, xs.reshape(bs, ts, D_MODEL), st("p_shift"), p_pool[:, :, 1:], wkv(p_wkv, bp),
            st("s_shift"), s_pool, wkv(s_wkv, bs))
```

```python
import functools

import jax
import jax.numpy as jnp
from jax import lax
from jax.experimental import pallas as pl
from jax.experimental.pallas import tpu as pltpu

F32 = jnp.float32
BF16 = jnp.bfloat16

D_MODEL = 1024
HEAD_DIM = 64
D_A = 512
N_HEADS = 8
LORA_WA = 128
LORA_G = 128
D_SHIFT = 3 * D_A + LORA_WA + LORA_G
D_POOL = 512
POOL_WINDOWS = (2, 4, 8, 16)
POOL_GD = 128
POOL_CARRY = 16
D_FF = 4096
RMS_EPS = 1e-6
GN_EPS = HEAD_DIM * 1e-5
L2_EPS = 1e-12

LANES = 128
SUBLANES = 8
MXU_DIM = 256
CHUNK = 64
PAIRS = D_A // LANES
VMEM_LIMIT = 56 * 1024 * 1024


def _dot(a, b):
    return jnp.dot(a.astype(BF16), b.astype(BF16), preferred_element_type=F32)


def _dot_nt(a, b):
    return lax.dot_general(a.astype(BF16), b.astype(BF16), (((1,), (1,)), ((), ())),
                           preferred_element_type=F32)


def _dot_tn(a, b):
    return lax.dot_general(a.astype(BF16), b.astype(BF16), (((0,), (0,)), ((), ())),
                           preferred_element_type=F32)


def _split3(x):
    hi = x.astype(BF16)
    r1 = x - hi.astype(F32)
    mid = r1.astype(BF16)
    lo = (r1 - mid.astype(F32)).astype(BF16)
    return hi, mid, lo


def _dot_sel(sel, x):
    hi, mid, lo = _split3(x)
    d = lambda p: jnp.dot(sel, p, preferred_element_type=F32)
    return d(hi) + d(mid) + d(lo)


def _seg_sum(x, ones_bd):
    hi = x.astype(BF16)
    lo = (x - hi.astype(F32)).astype(BF16)
    w = ones_bd.shape[0]
    d = lambda p: jnp.concatenate(
        [jnp.dot(p[:, i:i + w], ones_bd, preferred_element_type=F32) for i in range(0, x.shape[1], w)],
        axis=1)
    return d(hi) + d(lo)


def _div_pow2(x, d):
    assert d & (d - 1) == 0
    return jnp.right_shift(x, d.bit_length() - 1)


def _rms(x, g):
    return x * lax.rsqrt(jnp.mean(x * x, axis=-1, keepdims=True) + RMS_EPS) * g


def _sigmoid(x):
    return 1.0 / (1.0 + jnp.exp(-x))


def _softplus(x):
    return jnp.maximum(x, 0.0) + jnp.log(1.0 + jnp.exp(-jnp.abs(x)))


def _prep_kernel(x_ref, shift_ref, pool_ref, *rest, nb, tt, pos0, n_prev):
    rest = list(rest)
    pool_prev_ref = rest.pop(0) if n_prev else None
    (n1_ref, win_ref, mu_ref, dec0_ref, wdec_ref, a0_ref, wa_ref, wg_ref, kk_ref, ka_ref, ones_ref,
     wpool_ref, pscale_ref, wbup_ref,
     r_ref, lw_ref, k_ref, v_ref, a_ref, b_ref, g_ref, ga_ref, gbb_ref,
     shift_out_ref, pool_out_ref, cz_ref, cu_ref) = rest
    if n_prev:
        pool_out_ref[0:n_prev] = pool_prev_ref[...]
    tm = tt * nb
    cz = cz_ref.shape[0]
    cu = cu_ref.shape[0]
    t = pl.program_id(1)
    hist = cu // nb if nb == 1 else pool_ref.shape[0]

    def put(ref, val):
        if nb == 1:
            ref[...] = val.reshape(ref.shape)
        else:
            for i in range(tt):
                ref[:, i, :] = val[i * nb:(i + 1) * nb]

    @pl.when(t == 0)
    def _():
        cz_ref[...] = shift_ref[...].reshape(cz_ref.shape)
        past = pool_ref[...].reshape(hist * nb, D_POOL)
        if hist * nb < cu:
            past = jnp.concatenate([jnp.zeros((cu - hist * nb, D_POOL), F32), past], axis=0)
        cu_ref[...] = past

    if nb == 1:
        x = x_ref[...].reshape(tm, D_MODEL)
    else:
        x = jnp.concatenate([x_ref[:, i, :] for i in range(tt)], axis=0)
    hb = _rms(x, n1_ref[...]).astype(BF16)

    c0 = D_SHIFT + D_POOL
    z_rw = jnp.dot(hb, win_ref[:, 0:D_SHIFT], preferred_element_type=F32)
    u = jnp.dot(hb, win_ref[:, D_SHIFT:c0], preferred_element_type=F32)
    z_ga = jnp.dot(hb, win_ref[:, c0:c0 + D_MODEL], preferred_element_type=F32)
    z_gb = jnp.dot(hb, win_ref[:, c0 + D_MODEL:c0 + 2 * D_MODEL], preferred_element_type=F32)

    ext = jnp.concatenate([cz_ref[...], z_rw], axis=0)
    z_prev = pltpu.roll(ext, nb, 0)[cz:]
    zs = z_rw + (z_prev - z_rw) * mu_ref[...]
    new_cz = z_rw[tm - cz:]
    cz_ref[...] = new_cz
    shift_out_ref[...] = new_cz.reshape(shift_out_ref.shape)

    r = zs[:, 0:D_A]
    k = zs[:, D_A:2 * D_A]
    v = zs[:, 2 * D_A:3 * D_A]
    lwa = zs[:, 3 * D_A:3 * D_A + LORA_WA]
    lg = zs[:, 3 * D_A + LORA_WA:D_SHIFT]
    w_log = -_softplus(-(dec0_ref[...] + _dot(jnp.tanh(lwa), wdec_ref[...]))) - 0.5
    a_in = _sigmoid(a0_ref[...] + _dot(lwa, wa_ref[...]))
    kk = k * kk_ref[...]
    kk = kk / jnp.maximum(jnp.sqrt(_seg_sum(kk * kk, ones_ref[...])), L2_EPS)
    put(r_ref, r)
    put(lw_ref, -jnp.exp(w_log))
    put(k_ref, k * (1.0 + (a_in - 1.0) * ka_ref[...]))
    put(v_ref, v)
    put(a_ref, -kk)
    put(b_ref, kk * a_in)
    put(g_ref, _dot(_sigmoid(lg), wg_ref[...]))

    extu =jnp.concatenate([cu_ref[...], u], axis=0)
    new_cu = extu[tm:]
    cu_ref[...] = new_cu
    pool_out_ref[n_prev] = new_cu[cu - hist * nb:].reshape(pool_out_ref.shape[1:])
    row = lax.broadcasted_iota(jnp.int32, (tm, 1), 0)
    step = _div_pow2(row, nb)
    pos = pos0 + t * tt + step
    parts = []
    for gi, win in enumerate(POOL_WINDOWS):
        sl = slice(gi * POOL_GD, (gi + 1) * POOL_GD)
        s = extu[:, sl]
        span = 1
        while span < win:
            s = s + pltpu.roll(s, span * nb, 0)
            span *= 2
        cnt = jnp.minimum(pos + 1, win).astype(F32)
        p = s[cu:] / cnt - u[:, sl]
        parts.append(_dot(p, wpool_ref[gi]))
    pooled = jnp.concatenate(parts, axis=-1) * pscale_ref[...]
    b_out = _dot(pooled, wbup_ref[...])

    put(ga_ref, _sigmoid(z_ga))
    put(gbb_ref, _sigmoid(z_gb) * b_out)


def _layer_spec(w, l):
    li = min(l, w.shape[0] - 1)
    nd = w.ndim - 1
    return pl.BlockSpec((None,) + w.shape[1:], lambda *_: (li,) + (0,) * nd, pipeline_mode=pl.Buffered(1))


def _prep_call(x, shift_in, pool_in, pool_prev, wts, l, *, nb, tt, pos0, grid, row_block, row_map,
               shift_shape, shift_in_map, shift_out_map, pool_shape, pool_block, pool_in_spec, pool_stack_map):
    lead = x.shape[:-1]
    n_prev = 0 if pool_prev is None else pool_prev.shape[0]
    outs_rows = lambda w: jax.ShapeDtypeStruct(lead + (w,), F32)
    rb = lambda w: pl.BlockSpec(row_block + (w,), row_map)
    cz = max(SUBLANES, nb)
    cu = POOL_CARRY * nb
    shift_block = (1, cz, D_SHIFT)
    stacked = lambda n: pl.BlockSpec((n,) + pool_block, pool_stack_map)
    in_specs = ([rb(D_MODEL), pl.BlockSpec(shift_block, shift_in_map), pool_in_spec]
                + ([stacked(n_prev)] if n_prev else []) + [_layer_spec(w, l) for w in wts])
    out_shape = [outs_rows(D_A)] * 7 + [outs_rows(D_MODEL)] * 2 + [
        jax.ShapeDtypeStruct(shift_shape, F32), jax.ShapeDtypeStruct((n_prev + 1,) + pool_shape, F32)]
    out_specs = [rb(D_A)] * 7 + [rb(D_MODEL)] * 2 + [pl.BlockSpec(shift_block, shift_out_map),
                                                      stacked(n_prev + 1)]
    return pl.pallas_call(
        functools.partial(_prep_kernel, nb=nb, tt=tt, pos0=pos0, n_prev=n_prev),
        grid=grid, in_specs=in_specs, out_specs=out_specs, out_shape=out_shape,
        scratch_shapes=[pltpu.VMEM((cz, D_SHIFT), F32), pltpu.VMEM((cu, D_POOL), F32)],
        compiler_params=pltpu.CompilerParams(dimension_semantics=("arbitrary", "arbitrary"),
                                             vmem_limit_bytes=VMEM_LIMIT),
        name="prep",
    )(x, shift_in, pool_in, *([pool_prev] if n_prev else []), *wts)


def _scan_masks(steps):
    assert CHUNK == HEAD_DIM
    ri = lax.broadcasted_iota(jnp.int32, (CHUNK, LANES), 0)
    ci = jnp.bitwise_and(lax.broadcasted_iota(jnp.int32, (CHUNK, LANES), 1), CHUNK - 1)
    blk = lambda s: _div_pow2(ri, s) == _div_pow2(ci, s)
    same_seq = blk(steps)
    strict = same_seq & (ri > ci)
    incl = same_seq & (ri >= ci)
    levels = []
    size = SUBLANES
    while size < steps:
        size *= 2
        levels.append(strict & blk(size) & ~blk(size // 2))
    base = strict & blk(SUBLANES)
    eye = ri == ci
    return same_seq, strict, incl, base, levels, eye


def _bdot(a, b):
    return lax.dot_general(a.astype(BF16), b.astype(BF16), (((2,), (1,)), ((0,), (0,))),
                           preferred_element_type=F32)


def _bdot_nt(a, b):
    return lax.dot_general(a.astype(BF16), b.astype(BF16), (((2,), (2,)), ((0,), (0,))),
                           preferred_element_type=F32)


def _bdot_tn(a, b):
    return lax.dot_general(a.astype(BF16), b.astype(BF16), (((1,), (1,)), ((0,), (0,))),
                           preferred_element_type=F32)


def _head_stack(x):
    first = lax.broadcasted_iota(jnp.int32, (1, 1, LANES), 2) < HEAD_DIM
    return jnp.concatenate([jnp.where(first, x, 0.0), jnp.where(first, 0.0, x)], axis=1)


def _hdot(a, b):
    return _bdot(a, _head_stack(b))


def _tri_inverse(a_strict, base, levels, eye):
    ad = jnp.where(base, a_strict, 0.0)
    p2 = _hdot(ad, ad)
    t = jnp.where(eye, 1.0, ad)
    t = t + _hdot(t, p2)
    t = t + _hdot(t, _hdot(p2, p2))
    for lvl in levels:
        t = t + _hdot(t, _hdot(jnp.where(lvl, a_strict, 0.0), t))
    return t


def _scan_kernel(r_ref, lw_ref, k_ref, v_ref, a_ref, b_ref, h0_ref, *rest, steps, n_chunks, n_prev):
    y_ref, hout_ref, h_ref = rest[-3:]
    if n_prev:
        hout_ref[0:n_prev] = rest[0][...]
    nseq = CHUNK // steps
    nst = h_ref.shape[0]
    t = pl.program_id(1)

    @pl.when(t == 0)
    def _():
        zero = jnp.zeros((HEAD_DIM, HEAD_DIM), F32)
        for s in range(nst):
            for p in range(PAIRS):
                top = jnp.concatenate([h0_ref[0, s, p, 0], zero], axis=1)
                bot = jnp.concatenate([zero, h0_ref[0, s, p, 1]], axis=1)
                h_ref[s, p] = jnp.concatenate([top, bot], axis=0)

    same_seq, strict, incl, base, levels, eye = _scan_masks(steps)
    sel = jnp.concatenate([incl[:, :CHUNK], same_seq[:, :CHUNK]], axis=0).astype(BF16)
    hi = lax.broadcasted_iota(jnp.int32, (LANES, LANES), 0)
    hj = lax.broadcasted_iota(jnp.int32, (LANES, LANES), 1)
    same_head = _div_pow2(hi, HEAD_DIM) == _div_pow2(hj, HEAD_DIM)
    rows = n_chunks * CHUNK

    ld = lambda ref: ref[...].reshape(rows, D_A)
    r, lw, k, v, a, b = (ld(x) for x in (r_ref, lw_ref, k_ref, v_ref, a_ref, b_ref))
    cc = [_dot_sel(sel, lw[c * CHUNK:(c + 1) * CHUNK]) for c in range(n_chunks)]
    cs = jnp.concatenate([x[:CHUNK] for x in cc], axis=0)
    cl = jnp.concatenate([x[CHUNK:] for x in cc], axis=0)
    e_cs = jnp.exp(cs)
    e_ncs = jnp.exp(-cs)
    e_cl = jnp.exp(cl - cs)
    gam = jnp.exp(cl)

    def inst(x):
        x3 = x.reshape(n_chunks, CHUNK, D_A)
        return jnp.concatenate([x3[:, :, p * LANES:(p + 1) * LANES] for p in range(PAIRS)], axis=0)

    at = inst(a * jnp.exp(cs - lw))
    rt = inst(r * e_cs)
    bh = inst(b * e_cl)
    kh = inst(k * e_cl)
    vv = inst(v)
    aa = _bdot_nt(jnp.concatenate([at, rt], axis=1),
                  jnp.concatenate([_head_stack(inst(b * e_ncs)), _head_stack(inst(k * e_ncs))], axis=1))
    a_ab = jnp.where(strict, aa[:, :CHUNK, :LANES], 0.0)
    a_ak = jnp.where(strict, aa[:, :CHUNK, LANES:], 0.0)
    a_rb = jnp.where(incl, aa[:, CHUNK:, :LANES], 0.0)
    a_rk = jnp.where(incl, aa[:, CHUNK:, LANES:], 0.0)
    tinv = _tri_inverse(a_ab, base, levels, eye)
    akv = _hdot(jnp.concatenate([a_ak, a_rk], axis=1), vv)
    tx = _bdot(tinv, jnp.concatenate([_head_stack(at), _head_stack(akv[:, :CHUNK])], axis=2))
    ah = tx[:, :, :LANES]
    w1 = tx[:, :, LANES:]
    rx = _bdot(a_rb, jnp.concatenate([_head_stack(ah), _head_stack(w1)], axis=2))
    rh = rt + rx[:, :, :LANES]
    y1 = rx[:, :, LANES:] + akv[:, CHUNK:]

    ng = PAIRS * n_chunks
    bf16_rows = 2 * SUBLANES

    def seqsplit(x):
        parts = [x[:, s * steps:(s + 1) * steps] for s in range(nseq)]
        if steps < bf16_rows:
            pad = jnp.zeros((x.shape[0], bf16_rows - steps, x.shape[2]), F32)
            parts = [jnp.concatenate([q, pad], axis=1) for q in parts]
        return jnp.concatenate(parts, axis=0)

    bhs = seqsplit(bh)
    mp = jnp.where(same_head, _bdot_tn(seqsplit(ah), bhs), 0.0)
    np_ = jnp.where(same_head, _bdot_tn(jnp.concatenate([seqsplit(w1), seqsplit(vv)], axis=1),
                                        jnp.concatenate([bhs, seqsplit(kh)], axis=1)), 0.0)
    gam_i = inst(gam)
    gam_m = jnp.concatenate([gam_i[:, s * steps:s * steps + 1] for s in range(nseq)], axis=0)
    order = [(s, p, c) for s in range(nseq) for p in range(PAIRS) for c in range(n_chunks)]
    if nst == nseq:
        take = lambda x, c: jnp.concatenate(
            [x[s * ng + p * n_chunks + c][None] for s in range(nseq) for p in range(PAIRS)], axis=0)
        cur = jnp.concatenate([h_ref[s, p][None] for s in range(nseq) for p in range(PAIRS)], axis=0)
        starts = []
        for c in range(n_chunks):
            starts.append(cur)
            cur = cur * take(gam_m, c) + _bdot(cur, take(mp, c)) + take(np_, c)
        s_start = jnp.concatenate(
            [starts[c][s * PAIRS + p][None] for (s, p, c) in order], axis=0)
        finals = {(s, p): cur[s * PAIRS + p] for s in range(nseq) for p in range(PAIRS)}
    else:
        s_start = jnp.concatenate([h_ref[c * nseq + s, p][None] for (s, p, c) in order], axis=0)
        s_end = s_start * gam_m + _bdot(s_start, mp) + np_
        finals = {(c * nseq + s, p): s_end[i] for i, (s, p, c) in enumerate(order)}
    y = (_bdot_nt(seqsplit(rh), s_start) + seqsplit(y1))[:, :steps]
    for i, (s, p, c) in enumerate(order):
        r0 = c * CHUNK + s * steps
        if len(y_ref.shape) == 3:
            y_ref[0, r0:r0 + steps, p * LANES:(p + 1) * LANES] = y[i]
        else:
            y_ref[r0:r0 + steps, p * LANES:(p + 1) * LANES] = y[i]
    for (s, p), h in finals.items():
        h_ref[s, p] = h
        hout_ref[n_prev, 0, s, p, 0] = h[:HEAD_DIM, :HEAD_DIM]
        hout_ref[n_prev, 0, s, p, 1] = h[HEAD_DIM:, HEAD_DIM:]


def _scan_call(vecs, h0, prev, l, *, steps, n_chunks, grid, row_block, row_map):
    nst = h0.shape[2]
    li = min(l, h0.shape[0] - 1)
    n_prev = 0 if prev is None else prev.shape[0]
    st_block = (1, nst, PAIRS, 2, HEAD_DIM, HEAD_DIM)
    rb = pl.BlockSpec(row_block + (D_A,), row_map)
    hb = pl.BlockSpec((None,) + st_block, lambda b, t: (li, b, 0, 0, 0, 0, 0))
    stacked = lambda n: pl.BlockSpec((n,) + st_block, lambda b, t: (0, b, 0, 0, 0, 0, 0))
    return pl.pallas_call(
        functools.partial(_scan_kernel, steps=steps, n_chunks=n_chunks, n_prev=n_prev),
        grid=grid, in_specs=[rb] * 6 + [hb] + ([stacked(n_prev)] if n_prev else []),
        out_specs=[rb, stacked(n_prev + 1)],
        out_shape=[jax.ShapeDtypeStruct(vecs[0].shape, F32),
                   jax.ShapeDtypeStruct((n_prev + 1,) + h0.shape[1:], F32)],
        scratch_shapes=[pltpu.VMEM((nst, PAIRS, LANES, LANES), F32)],
        compiler_params=pltpu.CompilerParams(dimension_semantics=("arbitrary", "arbitrary"),
                                             vmem_limit_bytes=VMEM_LIMIT),
        name="scan",
    )(*vecs, h0, *([prev] if n_prev else []))


def _post_kernel(y_ref, r_ref, k_ref, v_ref, g_ref, ga_ref, gbb_ref, x_ref, ones_ref, rk_ref, lng_ref,
                 lnb_ref, waup_ref, wo_ref, n2_ref, w1_ref, w2_ref, fn_ref, o_ref, *, final):
    ones = ones_ref[...]
    y = y_ref[...]
    mu = _seg_sum(y, ones) * (1.0 / HEAD_DIM)
    d = y - mu
    var = _seg_sum(d * d, ones) * (1.0 / HEAD_DIM)
    yn = d * lax.rsqrt(var + GN_EPS) * lng_ref[...] + lnb_ref[...]
    bonus = _seg_sum(r_ref[...] * k_ref[...] * rk_ref[...], ones) * v_ref[...]
    a_out = _dot((yn + bonus) * g_ref[...], waup_ref[...])
    merged = ga_ref[...] * a_out + gbb_ref[...]
    x1 = x_ref[...] + _dot(merged, wo_ref[...])
    f = jnp.maximum(_dot(_rms(x1, n2_ref[...]), w1_ref[...]), 0.0)
    x2 = x1 + _dot(f * f, w2_ref[...])
    o_ref[...] = _rms(x2, fn_ref[...]) if final else x2


def _post_call(rows_in, wts, l, *, final, tm):
    n = rows_in[0].shape[0]
    rb = lambda a: pl.BlockSpec((tm, a.shape[-1]), lambda i: (i, 0))
    return pl.pallas_call(
        functools.partial(_post_kernel, final=final),
        grid=(n // tm,),
        in_specs=[rb(a) for a in rows_in] + [_layer_spec(w, l) for w in wts],
        out_specs=pl.BlockSpec((tm, D_MODEL), lambda i: (i, 0)),
        out_shape=jax.ShapeDtypeStruct((n, D_MODEL), F32),
        compiler_params=pltpu.CompilerParams(dimension_semantics=("arbitrary",),
                                             vmem_limit_bytes=VMEM_LIMIT),
        name="post",
    )(*rows_in, *wts)


def _stacked_weights(w_in, mu_shift, decay0, w_decay2, a0, w_a2, w_g2, k_k, k_a, r_k, ln_x_g, ln_x_b,
                     w_a_up, w_pool, pool_scale, w_b_up, w_o, norm1_g, norm2_g, w_ff1, w_ff2, final_norm_g):
    depth = w_in.shape[0]
    row = lambda a: a.reshape(a.shape[0], 1, -1)
    zpad = jnp.zeros((depth, LORA_WA // 2, D_A), F32)
    seg = jnp.arange(MXU_DIM) // HEAD_DIM
    ones_bd = (seg[:, None] == seg[None, :]).astype(BF16)[None]
    prep = [row(norm1_g), w_in.astype(BF16), row(mu_shift), row(decay0),
            jnp.concatenate([w_decay2, zpad], axis=1).astype(BF16), row(a0),
            jnp.concatenate([zpad, w_a2], axis=1).astype(BF16), w_g2.astype(BF16),
            row(k_k), row(k_a), ones_bd, w_pool.astype(BF16), row(pool_scale), w_b_up.astype(BF16)]
    post = [ones_bd, row(r_k), row(ln_x_g), row(ln_x_b), w_a_up.astype(BF16), w_o.astype(BF16),
            row(norm2_g), w_ff1.astype(BF16), w_ff2.astype(BF16), final_norm_g.reshape(1, 1, -1)]
    return prep, post


def kernel(x_prompt, x_sample, state_shift, state_pool, state_wkv, norm1_g, w_in, mu_shift, decay0,
           w_decay2, a0, w_a2, w_g2, k_k, k_a, r_k, ln_x_g, ln_x_b, w_a_up, w_pool, pool_scale, w_b_up,
           w_o, norm2_g, w_ff1, w_ff2, final_norm_g):
    depth = w_in.shape[0]
    bp, tp, _ = x_prompt.shape
    bs, ts, _ = x_sample.shape
    past_len = 16384
    tt_p = 256
    nb_s = 32
    ts_p = 512
    tm_post = 256
    nc_s = 2
    nseq = CHUNK // ts
    hist = state_pool.shape[2]
    prep_w, post_w = _stacked_weights(
        w_in, mu_shift, decay0, w_decay2, a0, w_a2, w_g2, k_k, k_a, r_k, ln_x_g, ln_x_b, w_a_up, w_pool,
        pool_scale, w_b_up, w_o, norm1_g, norm2_g, w_ff1, w_ff2, final_norm_g)

    p_shift0 = jnp.zeros((bp, SUBLANES, D_SHIFT), F32)
    p_pool0 = jnp.zeros((bp, POOL_CARRY, D_POOL), F32)
    p_wkv0 = jnp.zeros((1, bp, 1, PAIRS, 2, HEAD_DIM, HEAD_DIM), F32)
    s_pool0 = jnp.swapaxes(state_pool, 1, 2)
    s_wkv0 = state_wkv.reshape(depth, bs // (nseq * nc_s), nseq * nc_s, PAIRS, 2, HEAD_DIM, HEAD_DIM)

    xp = x_prompt
    xs = x_sample.reshape(bs * ts, D_MODEL)
    outs = {k: [] for k in ("p_shift", "s_shift")}
    p_pool = p_wkv = s_pool = s_wkv = None
    for l in range(depth):
        final = l == depth - 1

        res = _prep_call(
            xp, p_shift0, p_pool0, p_pool, prep_w, l, nb=1, tt=tt_p, pos0=0, grid=(bp, tp // tt_p),
            row_block=(1, tt_p), row_map=lambda b, t: (b, t, 0),
            shift_shape=p_shift0.shape, shift_in_map=lambda b, t: (b, 0, 0),
            shift_out_map=lambda b, t: (b, 0, 0), pool_shape=p_pool0.shape,
            pool_block=(1, POOL_CARRY, D_POOL),
            pool_in_spec=pl.BlockSpec((1, POOL_CARRY, D_POOL), lambda b, t: (b, 0, 0)),
            pool_stack_map=lambda b, t: (0, b, 0, 0))
        r, lw, k, v, a, b, g, ga, gbb, sh, p_pool = res
        outs["p_shift"].append(sh[:, SUBLANES - 1])
        y, p_wkv = _scan_call(
            (r, lw, k, v, a, b), p_wkv0, p_wkv, l, steps=CHUNK, n_chunks=ts_p // CHUNK,
            grid=(bp, tp // ts_p), row_block=(1, ts_p), row_map=lambda b, t: (b, t, 0))
        flat = lambda z: z.reshape(bp * tp, z.shape[-1])
        xp = _post_call([flat(z) for z in (y, r, k, v, g, ga, gbb, xp)], post_w, l, final=final,
                        tm=tm_post).reshape(bp, tp, D_MODEL)

        res = _prep_call(
            xs.reshape(bs, ts, D_MODEL), state_shift, s_pool0, s_pool, prep_w, l, nb=nb_s, tt=ts,
            pos0=past_len, grid=(bs // nb_s, 1), row_block=(nb_s, ts), row_map=lambda b, t: (b, 0, 0),
            shift_shape=(1, bs, D_SHIFT), shift_in_map=lambda b, t, l=l: (l, b, 0),
            shift_out_map=lambda b, t: (0, b, 0), pool_shape=(hist, bs, D_POOL),
            pool_block=(hist, nb_s, D_POOL),
            pool_in_spec=pl.BlockSpec((None, hist, nb_s, D_POOL), lambda b, t, l=l: (l, 0, b, 0)),
            pool_stack_map=lambda b, t: (0, 0, b, 0))
        r, lw, k, v, a, b, g, ga, gbb = (z.reshape(bs * ts, z.shape[-1]) for z in res[:9])
        outs["s_shift"].append(res[9][0])
        s_pool = res[10]
        y, s_wkv = _scan_call(
            (r, lw, k, v, a, b), s_wkv0, s_wkv, l, steps=ts, n_chunks=nc_s,
            grid=(bs // (nseq * nc_s), 1), row_block=(nc_s * CHUNK,), row_map=lambda b, t: (b, 0))
        xs = _post_call([y, r, k, v, g, ga, gbb, xs], post_w, l, final=final, tm=tm_post)

    st = lambda key: jnp.stack(outs[key])
    wkv = lambda h, n: h.reshape(depth, n, N_HEADS, HEAD_DIM, HEAD_DIM)
    return (xp, xs.reshape(bs, ts, D_MODEL), st("p_shift"), p_pool[:, :, 1:], wkv(p_wkv, bp),
            st("s_shift"), jnp.swapaxes(s_pool, 1, 2), wkv(s_wkv, bs))
```

```python
import functools

import jax
import jax.numpy as jnp
from jax import lax
from jax.experimental import pallas as pl
from jax.experimental.pallas import tpu as pltpu

F32 = jnp.float32
BF16 = jnp.bfloat16

D_MODEL = 1024
HEAD_DIM = 64
D_A = 512
N_HEADS = 8
LORA_WA = 128
LORA_G = 128
D_SHIFT = 3 * D_A + LORA_WA + LORA_G
D_POOL = 512
POOL_WINDOWS = (2, 4, 8, 16)
POOL_GD = 128
POOL_CARRY = 16
D_FF = 4096
PAST_LEN = 16384
RMS_EPS = 1e-6
GN_EPS = HEAD_DIM * 1e-5
L2_EPS = 1e-12

LANES = 128
SUBLANES = 8
MXU_DIM = 256
CHUNK = 64
FF_CHUNK = 1024
PAIRS = D_A // LANES
VMEM_LIMIT = 56 * 1024 * 1024


def _dot(a, b):
    return jnp.dot(a.astype(BF16), b.astype(BF16), preferred_element_type=F32)


def _split3(x):
    hi = x.astype(BF16)
    r1 = x - hi.astype(F32)
    mid = r1.astype(BF16)
    lo = (r1 - mid.astype(F32)).astype(BF16)
    return hi, mid, lo


def _dot_sel(sel, x):
    hi, mid, lo = _split3(x)
    d = lambda p: jnp.dot(sel, p, preferred_element_type=F32)
    return d(hi) + d(mid) + d(lo)


def _seg_sum(x, ones_bd):
    hi = x.astype(BF16)
    lo = (x - hi.astype(F32)).astype(BF16)
    w = ones_bd.shape[0]
    d = lambda p: jnp.concatenate(
        [jnp.dot(p[:, i:i + w], ones_bd, preferred_element_type=F32) for i in range(0, x.shape[1], w)],
        axis=1)
    return d(hi) + d(lo)


def _div_pow2(x, d):
    assert d & (d - 1) == 0
    return jnp.right_shift(x, d.bit_length() - 1)


def _rms(x, g):
    return x * lax.rsqrt(jnp.mean(x * x, axis=-1, keepdims=True) + RMS_EPS) * g


def _sigmoid(x):
    return 1.0 / (1.0 + jnp.exp(-x))


def _softplus(x):
    return jnp.maximum(x, 0.0) + jnp.log(1.0 + jnp.exp(-jnp.abs(x)))


def _prep_kernel(x_ref, shift_ref, pool_ref, *rest, nb, tt, pos0, n_prev):
    rest = list(rest)
    pool_prev_ref = rest.pop(0) if n_prev else None
    (n1_ref, win_ref, mu_ref, dec0_ref, wdec_ref, a0_ref, wa_ref, wg_ref, kk_ref, ka_ref, ones_ref,
     wpool_ref, pscale_ref, wbup_ref,
     r_ref, lw_ref, k_ref, v_ref, a_ref, b_ref, g_ref, ga_ref, gbb_ref,
     shift_out_ref, pool_out_ref, cz_ref, cu_ref) = rest
    if n_prev:
        pool_out_ref[0:n_prev] = pool_prev_ref[...]
    tm = tt * nb
    cz = cz_ref.shape[0]
    cu = cu_ref.shape[0]
    t = pl.program_id(1)
    hist = cu // nb if nb == 1 else pool_ref.shape[0]

    def put(ref, val):
        if nb == 1:
            ref[...] = val.reshape(ref.shape)
        else:
            for i in range(tt):
                ref[:, i, :] = val[i * nb:(i + 1) * nb]

    @pl.when(t == 0)
    def _():
        cz_ref[...] = shift_ref[...].reshape(cz_ref.shape)
        past = pool_ref[...].reshape(hist * nb, D_POOL)
        if hist * nb < cu:
            past = jnp.concatenate([jnp.zeros((cu - hist * nb, D_POOL), F32), past], axis=0)
        cu_ref[...] = past

    if nb == 1:
        x = x_ref[...].reshape(tm, D_MODEL)
    else:
        x = jnp.concatenate([x_ref[:, i, :] for i in range(tt)], axis=0)
    hb = _rms(x, n1_ref[...]).astype(BF16)

    c0 = D_SHIFT + D_POOL
    z_rw = jnp.dot(hb, win_ref[:, 0:D_SHIFT], preferred_element_type=F32)
    u = jnp.dot(hb, win_ref[:, D_SHIFT:c0], preferred_element_type=F32)
    z_ga = jnp.dot(hb, win_ref[:, c0:c0 + D_MODEL], preferred_element_type=F32)
    z_gb = jnp.dot(hb, win_ref[:, c0 + D_MODEL:c0 + 2 * D_MODEL], preferred_element_type=F32)

    ext = jnp.concatenate([cz_ref[...], z_rw], axis=0)
    z_prev = pltpu.roll(ext, nb, 0)[cz:]
    zs = z_rw + (z_prev - z_rw) * mu_ref[...]
    new_cz = z_rw[tm - cz:]
    cz_ref[...] = new_cz
    shift_out_ref[...] = new_cz.reshape(shift_out_ref.shape)

    r = zs[:, 0:D_A]
    k = zs[:, D_A:2 * D_A]
    v = zs[:, 2 * D_A:3 * D_A]
    lwa = zs[:, 3 * D_A:3 * D_A + LORA_WA]
    lg = zs[:, 3 * D_A + LORA_WA:D_SHIFT]
    w_log = -_softplus(-(dec0_ref[...] + _dot(jnp.tanh(lwa), wdec_ref[...]))) - 0.5
    a_in = _sigmoid(a0_ref[...] + _dot(lwa, wa_ref[...]))
    kk = k * kk_ref[...]
    kk = kk / jnp.maximum(jnp.sqrt(_seg_sum(kk * kk, ones_ref[...])), L2_EPS)
    put(r_ref, r)
    put(lw_ref, -jnp.exp(w_log))
    put(k_ref, k * (1.0 + (a_in - 1.0) * ka_ref[...]))
    put(v_ref, v)
    put(a_ref, -kk)
    put(b_ref, kk * a_in)
    put(g_ref, _dot(_sigmoid(lg), wg_ref[...]))

    extu = jnp.concatenate([cu_ref[...], u], axis=0)
    new_cu = extu[tm:]
    cu_ref[...] = new_cu
    pool_out_ref[n_prev] = new_cu[cu - hist * nb:].reshape(pool_out_ref.shape[1:])
    row = lax.broadcasted_iota(jnp.int32, (tm, 1), 0)
    step = _div_pow2(row, nb)
    pos = pos0 + t * tt + step
    parts = []
    for gi, win in enumerate(POOL_WINDOWS):
        sl = slice(gi * POOL_GD, (gi + 1) * POOL_GD)
        s = extu[:, sl]
        span = 1
        while span < win:
            s = s + pltpu.roll(s, span * nb, 0)
            span *= 2
        cnt = jnp.minimum(pos + 1, win).astype(F32)
        p = s[cu:] / cnt - u[:, sl]
        parts.append(_dot(p, wpool_ref[gi]))
    pooled = jnp.concatenate(parts, axis=-1) * pscale_ref[...]
    b_out = _dot(pooled, wbup_ref[...])

    put(ga_ref, _sigmoid(z_ga))
    put(gbb_ref, _sigmoid(z_gb) * b_out)


def _layer_spec(w, l):
    li = min(l, w.shape[0] - 1)
    nd = w.ndim - 1
    return pl.BlockSpec((None,) + w.shape[1:], lambda *_: (li,) + (0,) * nd, pipeline_mode=pl.Buffered(1))


def _prep_call(x, shift_in, pool_in, pool_prev, wts, l, *, nb, tt, pos0, grid, row_block, row_map,
               shift_shape, shift_in_map, shift_out_map, pool_shape, pool_block, pool_in_spec, pool_stack_map):
    lead = x.shape[:-1]
    n_prev = 0 if pool_prev is None else pool_prev.shape[0]
    outs_rows = lambda w: jax.ShapeDtypeStruct(lead + (w,), F32)
    rb = lambda w: pl.BlockSpec(row_block + (w,), row_map)
    cz = max(SUBLANES, nb)
    cu = POOL_CARRY * nb
    shift_block = (1, cz, D_SHIFT)
    stacked = lambda n: pl.BlockSpec((n,) + pool_block, pool_stack_map)
    in_specs = ([rb(D_MODEL), pl.BlockSpec(shift_block, shift_in_map), pool_in_spec]
                + ([stacked(n_prev)] if n_prev else []) + [_layer_spec(w, l) for w in wts])
    out_shape = [outs_rows(D_A)] * 7 + [outs_rows(D_MODEL)] * 2 + [
        jax.ShapeDtypeStruct(shift_shape, F32), jax.ShapeDtypeStruct((n_prev + 1,) + pool_shape, F32)]
    out_specs = [rb(D_A)] * 7 + [rb(D_MODEL)] * 2 + [pl.BlockSpec(shift_block, shift_out_map),
                                                      stacked(n_prev + 1)]
    return pl.pallas_call(
        functools.partial(_prep_kernel, nb=nb, tt=tt, pos0=pos0, n_prev=n_prev),
        grid=grid, in_specs=in_specs, out_specs=out_specs, out_shape=out_shape,
        scratch_shapes=[pltpu.VMEM((cz, D_SHIFT), F32), pltpu.VMEM((cu, D_POOL), F32)],
        compiler_params=pltpu.CompilerParams(dimension_semantics=("arbitrary", "arbitrary"),
                                             vmem_limit_bytes=VMEM_LIMIT),
        name="prep",
    )(x, shift_in, pool_in, *([pool_prev] if n_prev else []), *wts)


def _scan_masks(steps):
    assert CHUNK == HEAD_DIM
    ri = lax.broadcasted_iota(jnp.int32, (CHUNK, LANES), 0)
    ci = jnp.bitwise_and(lax.broadcasted_iota(jnp.int32, (CHUNK, LANES), 1), CHUNK - 1)
    blk = lambda s: _div_pow2(ri, s) == _div_pow2(ci, s)
    same_seq = blk(steps)
    strict = same_seq & (ri > ci)
    incl = same_seq & (ri >= ci)
    levels = []
    size = SUBLANES
    while size < steps:
        size *= 2
        levels.append(strict & blk(size) & ~blk(size // 2))
    base = strict & blk(SUBLANES)
    eye = ri == ci
    return same_seq, strict, incl, base, levels, eye


def _bdot(a, b):
    return lax.dot_general(a.astype(BF16), b.astype(BF16), (((2,), (1,)), ((0,), (0,))),
                           preferred_element_type=F32)


def _bdot_nt(a, b):
    return lax.dot_general(a.astype(BF16), b.astype(BF16), (((2,), (2,)), ((0,), (0,))),
                           preferred_element_type=F32)


def _bdot_tn(a, b):
    return lax.dot_general(a.astype(BF16), b.astype(BF16), (((1,), (1,)), ((0,), (0,))),
                           preferred_element_type=F32)


def _head_stack(x):
    first = lax.broadcasted_iota(jnp.int32, (1, 1, LANES), 2) < HEAD_DIM
    return jnp.concatenate([jnp.where(first, x, 0.0), jnp.where(first, 0.0, x)], axis=1)


def _hdot(a, b):
    return _bdot(a, _head_stack(b))


def _tri_inverse(a_strict, base, levels, eye):
    ad = jnp.where(base, a_strict, 0.0)
    p2 = _hdot(ad, ad)
    t = jnp.where(eye, 1.0, ad)
    t = t + _hdot(t, p2)
    t = t + _hdot(t, _hdot(p2, p2))
    for lvl in levels:
        t = t + _hdot(t, _hdot(jnp.where(lvl, a_strict, 0.0), t))
    return t


def _scan_kernel(r_ref, lw_ref, k_ref, v_ref, a_ref, b_ref, h0_ref, *rest, steps, n_chunks, n_prev):
    y_ref, hout_ref, h_ref = rest[-3:]
    if n_prev:
        hout_ref[0:n_prev] = rest[0][...]
    nseq = CHUNK // steps
    nst = h_ref.shape[0]
    t = pl.program_id(1)

    @pl.when(t == 0)
    def _():
        zero = jnp.zeros((HEAD_DIM, HEAD_DIM), F32)
        for s in range(nst):
            for p in range(PAIRS):
                top = jnp.concatenate([h0_ref[0, s, p, 0], zero], axis=1)
                bot = jnp.concatenate([zero, h0_ref[0, s, p, 1]], axis=1)
                h_ref[s, p] = jnp.concatenate([top, bot], axis=0)

    same_seq, strict, incl, base, levels, eye = _scan_masks(steps)
    sel = jnp.concatenate([incl[:, :CHUNK], same_seq[:, :CHUNK]], axis=0).astype(BF16)
    hi = lax.broadcasted_iota(jnp.int32, (LANES, LANES), 0)
    hj = lax.broadcasted_iota(jnp.int32, (LANES, LANES), 1)
    same_head = _div_pow2(hi, HEAD_DIM) == _div_pow2(hj, HEAD_DIM)
    rows = n_chunks * CHUNK

    ld = lambda ref: ref[...].reshape(rows, D_A)
    r, lw, k, v, a, b = (ld(x) for x in (r_ref, lw_ref, k_ref, v_ref, a_ref, b_ref))
    cc = [_dot_sel(sel, lw[c * CHUNK:(c + 1) * CHUNK]) for c in range(n_chunks)]
    cs = jnp.concatenate([x[:CHUNK] for x in cc], axis=0)
    cl = jnp.concatenate([x[CHUNK:] for x in cc], axis=0)
    e_cs = jnp.exp(cs)
    e_ncs = jnp.exp(-cs)
    e_cl = jnp.exp(cl - cs)
    gam = jnp.exp(cl)

    def inst(x):
        x3 = x.reshape(n_chunks, CHUNK, D_A)
        return jnp.concatenate([x3[:, :, p * LANES:(p + 1) * LANES] for p in range(PAIRS)], axis=0)

    at = inst(a * jnp.exp(cs - lw))
    rt = inst(r * e_cs)
    bh = inst(b * e_cl)
    kh = inst(k * e_cl)
    vv = inst(v)
    aa = _bdot_nt(jnp.concatenate([at, rt], axis=1),
                  jnp.concatenate([_head_stack(inst(b * e_ncs)), _head_stack(inst(k * e_ncs))], axis=1))
    a_ab = jnp.where(strict, aa[:, :CHUNK, :LANES], 0.0)
    a_ak = jnp.where(strict, aa[:, :CHUNK, LANES:], 0.0)
    a_rb = jnp.where(incl, aa[:, CHUNK:, :LANES], 0.0)
    a_rk = jnp.where(incl, aa[:, CHUNK:, LANES:], 0.0)
    tinv = _tri_inverse(a_ab, base, levels, eye)
    akv = _hdot(jnp.concatenate([a_ak, a_rk], axis=1), vv)
    tx = _bdot(tinv, jnp.concatenate([_head_stack(at), _head_stack(akv[:, :CHUNK])], axis=2))
    ah = tx[:, :, :LANES]
    w1 = tx[:, :, LANES:]
    rx = _bdot(a_rb, jnp.concatenate([_head_stack(ah), _head_stack(w1)], axis=2))
    rh = rt + rx[:, :, :LANES]
    y1 = rx[:, :, LANES:] + akv[:, CHUNK:]

    ng = PAIRS * n_chunks
    bf16_rows = 2 * SUBLANES

    def seqsplit(x):
        parts = [x[:, s * steps:(s + 1) * steps] for s in range(nseq)]
        if steps < bf16_rows:
            pad = jnp.zeros((x.shape[0], bf16_rows - steps, x.shape[2]), F32)
            parts = [jnp.concatenate([q, pad], axis=1) for q in parts]
        return jnp.concatenate(parts, axis=0)

    bhs = seqsplit(bh)
    mp = jnp.where(same_head, _bdot_tn(seqsplit(ah), bhs), 0.0)
    np_ = jnp.where(same_head, _bdot_tn(jnp.concatenate([seqsplit(w1), seqsplit(vv)], axis=1),
                                        jnp.concatenate([bhs, seqsplit(kh)], axis=1)), 0.0)
    gam_i = inst(gam)
    gam_m = jnp.concatenate([gam_i[:, s * steps:s * steps + 1] for s in range(nseq)], axis=0)
    order = [(s, p, c) for s in range(nseq) for p in range(PAIRS) for c in range(n_chunks)]
    if nst == nseq:
        take = lambda x, c: jnp.concatenate(
            [x[s * ng + p * n_chunks + c][None] for s in range(nseq) for p in range(PAIRS)], axis=0)
        cur = jnp.concatenate([h_ref[s, p][None] for s in range(nseq) for p in range(PAIRS)], axis=0)
        starts = []
        for c in range(n_chunks):
            starts.append(cur)
            cur = cur * take(gam_m, c) + _bdot(cur, take(mp, c)) + take(np_, c)
        s_start = jnp.concatenate(
            [starts[c][s * PAIRS + p][None] for (s, p, c) in order], axis=0)
        finals = {(s, p): cur[s * PAIRS + p] for s in range(nseq) for p in range(PAIRS)}
    else:
        s_start = jnp.concatenate([h_ref[c * nseq + s, p][None] for (s, p, c) in order], axis=0)
        s_end = s_start * gam_m + _bdot(s_start, mp) + np_
        finals = {(c * nseq + s, p): s_end[i] for i, (s, p, c) in enumerate(order)}
    y = (_bdot_nt(seqsplit(rh), s_start) + seqsplit(y1))[:, :steps]
    for i, (s, p, c) in enumerate(order):
        r0 = c * CHUNK + s * steps
        if len(y_ref.shape) == 3:
            y_ref[0, r0:r0 + steps, p * LANES:(p + 1) * LANES] = y[i]
        else:
            y_ref[r0:r0 + steps, p * LANES:(p + 1) * LANES] = y[i]
    for (s, p), h in finals.items():
        h_ref[s, p] = h
        hout_ref[n_prev, 0, s, p, 0] = h[:HEAD_DIM, :HEAD_DIM]
        hout_ref[n_prev, 0, s, p, 1] = h[HEAD_DIM:, HEAD_DIM:]


def _scan_call(vecs, h0, prev, l, *, steps, n_chunks, grid, row_block, row_map):
    nst = h0.shape[2]
    li = min(l, h0.shape[0] - 1)
    n_prev = 0 if prev is None else prev.shape[0]
    st_block = (1, nst, PAIRS, 2, HEAD_DIM, HEAD_DIM)
    rb = pl.BlockSpec(row_block + (D_A,), row_map)
    hb = pl.BlockSpec((None,) + st_block, lambda b, t: (li, b, 0, 0, 0, 0, 0))
    stacked = lambda n: pl.BlockSpec((n,) + st_block, lambda b, t: (0, b, 0, 0, 0, 0, 0))
    return pl.pallas_call(
        functools.partial(_scan_kernel, steps=steps, n_chunks=n_chunks, n_prev=n_prev),
        grid=grid, in_specs=[rb] * 6 + [hb] + ([stacked(n_prev)] if n_prev else []),
        out_specs=[rb, stacked(n_prev + 1)],
        out_shape=[jax.ShapeDtypeStruct(vecs[0].shape, F32),
                   jax.ShapeDtypeStruct((n_prev + 1,) + h0.shape[1:], F32)],
        scratch_shapes=[pltpu.VMEM((nst, PAIRS, LANES, LANES), F32)],
        compiler_params=pltpu.CompilerParams(dimension_semantics=("arbitrary", "arbitrary"),
                                             vmem_limit_bytes=VMEM_LIMIT),
        name="scan",
    )(*vecs, h0, *([prev] if n_prev else []))


def _post_kernel(y_ref, r_ref, k_ref, v_ref, g_ref, ga_ref, gbb_ref, x_ref, ones_ref, rk_ref, lng_ref,
                 lnb_ref, waup_ref, wo_ref, n2_ref, w1_ref, w2_ref, fn_ref, o_ref, *, final):
    ones = ones_ref[...]
    y = y_ref[...]
    mu = _seg_sum(y, ones) * (1.0 / HEAD_DIM)
    d = y - mu
    var = _seg_sum(d * d, ones) * (1.0 / HEAD_DIM)
    yn = d * lax.rsqrt(var + GN_EPS) * lng_ref[...] + lnb_ref[...]
    bonus = _seg_sum(r_ref[...] * k_ref[...] * rk_ref[...], ones) * v_ref[...]
    a_out = _dot((yn + bonus) * g_ref[...], waup_ref[...])
    merged = ga_ref[...] * a_out + gbb_ref[...]
    x1 = x_ref[...] + _dot(merged, wo_ref[...])
    hb = _rms(x1, n2_ref[...]).astype(BF16)
    x2 = x1
    for c in range(0, D_FF, FF_CHUNK):
        f = jnp.maximum(jnp.dot(hb, w1_ref[:, c:c + FF_CHUNK], preferred_element_type=F32), 0.0)
        x2 = x2 + _dot(f * f, w2_ref[c:c + FF_CHUNK, :])
    o_ref[...] = _rms(x2, fn_ref[...]) if final else x2


def _post_call(rows_in, wts, l, *, final, tm):
    n = rows_in[0].shape[0]
    rb = lambda a: pl.BlockSpec((tm, a.shape[-1]), lambda i: (i, 0))
    return pl.pallas_call(
        functools.partial(_post_kernel, final=final),
        grid=(n // tm,),
        in_specs=[rb(a) for a in rows_in] + [_layer_spec(w, l) for w in wts],
        out_specs=pl.BlockSpec((tm, D_MODEL), lambda i: (i, 0)),
        out_shape=jax.ShapeDtypeStruct((n, D_MODEL), F32),
        compiler_params=pltpu.CompilerParams(dimension_semantics=("arbitrary",),
                                             vmem_limit_bytes=VMEM_LIMIT),
        name="post",
    )(*rows_in, *wts)


def _stacked_weights(w_in, mu_shift, decay0, w_decay2, a0, w_a2, w_g2, k_k, k_a, r_k, ln_x_g, ln_x_b,
                     w_a_up, w_pool, pool_scale, w_b_up, w_o, norm1_g, norm2_g, w_ff1, w_ff2, final_norm_g):
    depth = w_in.shape[0]
    row = lambda a: a.reshape(a.shape[0], 1, -1)
    zpad = jnp.zeros((depth, LORA_WA // 2, D_A), F32)
    seg = jnp.arange(MXU_DIM) // HEAD_DIM
    ones_bd = (seg[:, None] == seg[None, :]).astype(BF16)[None]
    prep = [row(norm1_g), w_in.astype(BF16), row(mu_shift), row(decay0),
            jnp.concatenate([w_decay2, zpad], axis=1).astype(BF16), row(a0),
            jnp.concatenate([zpad, w_a2], axis=1).astype(BF16), w_g2.astype(BF16),
            row(k_k), row(k_a), ones_bd, w_pool.astype(BF16), row(pool_scale), w_b_up.astype(BF16)]
    post = [ones_bd, row(r_k), row(ln_x_g), row(ln_x_b), w_a_up.astype(BF16), w_o.astype(BF16),
            row(norm2_g), w_ff1.astype(BF16), w_ff2.astype(BF16), final_norm_g.reshape(1, 1, -1)]
    return prep, post


def kernel(x_prompt, x_sample, state_shift, state_pool, state_wkv, norm1_g, w_in, mu_shift, decay0,
           w_decay2, a0, w_a2, w_g2, k_k, k_a, r_k, ln_x_g, ln_x_b, w_a_up, w_pool, pool_scale, w_b_up,
           w_o, norm2_g, w_ff1, w_ff2, final_norm_g):
    depth = w_in.shape[0]
    bp, tp, _ = x_prompt.shape
    bs, ts, _ = x_sample.shape
    tt_p = 256
    nb_s = 32
    ts_p = 512
    tm_post = 512
    nc_s = 2
    nseq = CHUNK // ts
    hist = state_pool.shape[2]
    prep_w, post_w = _stacked_weights(
        w_in, mu_shift, decay0, w_decay2, a0, w_a2, w_g2, k_k, k_a, r_k, ln_x_g, ln_x_b, w_a_up, w_pool,
        pool_scale, w_b_up, w_o, norm1_g, norm2_g, w_ff1, w_ff2, final_norm_g)

    p_shift0 = jnp.zeros((bp, SUBLANES, D_SHIFT), F32)
    p_pool0 = jnp.zeros((bp, POOL_CARRY, D_POOL), F32)
    p_wkv0 = jnp.zeros((1, bp, 1, PAIRS, 2, HEAD_DIM, HEAD_DIM), F32)
    s_pool0 = jnp.swapaxes(state_pool, 1, 2)
    s_wkv0 = state_wkv.reshape(depth, bs // (nseq * nc_s), nseq * nc_s, PAIRS, 2, HEAD_DIM, HEAD_DIM)

    xp = x_prompt
    xs = x_sample.reshape(bs * ts, D_MODEL)
    outs = {k: [] for k in ("p_shift", "s_shift")}
    p_pool = p_wkv = s_pool = s_wkv = None
    for l in range(depth):
        final = l == depth - 1

        res = _prep_call(
            xp, p_shift0, p_pool0, p_pool, prep_w, l, nb=1, tt=tt_p, pos0=0, grid=(bp, tp // tt_p),
            row_block=(1, tt_p), row_map=lambda b, t: (b, t, 0),
            shift_shape=p_shift0.shape, shift_in_map=lambda b, t: (b, 0, 0),
            shift_out_map=lambda b, t: (b, 0, 0), pool_shape=p_pool0.shape,
            pool_block=(1, POOL_CARRY, D_POOL),
            pool_in_spec=pl.BlockSpec((1, POOL_CARRY, D_POOL), lambda b, t: (b, 0, 0)),
            pool_stack_map=lambda b, t: (0, b, 0, 0))
        r, lw, k, v, a, b, g, ga, gbb, sh, p_pool = res
        outs["p_shift"].append(sh[:, SUBLANES - 1])
        y, p_wkv = _scan_call(
            (r, lw, k, v, a, b), p_wkv0, p_wkv, l, steps=CHUNK, n_chunks=ts_p // CHUNK,
            grid=(bp, tp // ts_p), row_block=(1, ts_p), row_map=lambda b, t: (b, t, 0))
        flat = lambda z: z.reshape(bp * tp, z.shape[-1])
        xp = _post_call([flat(z) for z in (y, r, k, v, g, ga, gbb, xp)], post_w, l, final=final,
                        tm=tm_post).reshape(bp, tp, D_MODEL)

        res = _prep_call(
            xs.reshape(bs, ts, D_MODEL), state_shift, s_pool0, s_pool, prep_w, l, nb=nb_s, tt=ts,
            pos0=PAST_LEN, grid=(bs // nb_s, 1), row_block=(nb_s, ts), row_map=lambda b, t: (b, 0, 0),
            shift_shape=(1, bs, D_SHIFT), shift_in_map=lambda b, t, l=l: (l, b, 0),
            shift_out_map=lambda b, t: (0, b, 0), pool_shape=(hist, bs, D_POOL),
            pool_block=(hist, nb_s, D_POOL),
            pool_in_spec=pl.BlockSpec((None, hist, nb_s, D_POOL), lambda b, t, l=l: (l, 0, b, 0)),
            pool_stack_map=lambda b, t: (0, 0, b, 0))
        r, lw, k, v, a, b, g, ga, gbb = (z.reshape(bs * ts, z.shape[-1]) for z in res[:9])
        outs["s_shift"].append(res[9][0])
        s_pool = res[10]
        y, s_wkv = _scan_call(
            (r, lw, k, v, a, b), s_wkv0, s_wkv, l, steps=ts, n_chunks=nc_s,
            grid=(bs // (nseq * nc_s), 1), row_block=(nc_s * CHUNK,), row_map=lambda b, t: (b, 0))
        xs = _post_call([y, r, k, v, g, ga, gbb, xs], post_w, l, final=final, tm=tm_post)

    st = lambda key: jnp.stack(outs[key])
    wkv = lambda h, n: h.reshape(depth, n, N_HEADS, HEAD_DIM, HEAD_DIM)
    return (xp, xs.reshape(bs, ts, D_MODEL), st("p_shift"), p_pool[:, :, 1:], wkv(p_wkv, bp),
            st("s_shift"), jnp.swapaxes(s_pool, 1, 2), wkv(s_wkv, bs))
```

```python
import functools

import jax
import jax.numpy as jnp
from jax import lax
from jax.experimental import pallas as pl
from jax.experimental.pallas import tpu as pltpu

F32 = jnp.float32
BF16 = jnp.bfloat16

D_MODEL = 1024
HEAD_DIM = 64
D_A = 512
N_HEADS = 8
LORA_WA = 128
LORA_G = 128
D_SHIFT = 3 * D_A + LORA_WA + LORA_G
D_POOL = 512
POOL_WINDOWS = (2, 4, 8, 16)
POOL_GD = 128
POOL_CARRY = 16
D_FF = 4096
PAST_LEN = 16384
RMS_EPS = 1e-6
GN_EPS = HEAD_DIM * 1e-5
L2_EPS = 1e-12

LANES = 128
SUBLANES = 8
MXU_DIM = 256
CHUNK = 64
FF_CHUNK = 1024
PAIRS = D_A // LANES
VMEM_LIMIT = 56 * 1024 * 1024


def _dot(a, b):
    return jnp.dot(a.astype(BF16), b.astype(BF16), preferred_element_type=F32)


def _split3(x):
    hi = x.astype(BF16)
    r1 = x - hi.astype(F32)
    mid = r1.astype(BF16)
    lo = (r1 - mid.astype(F32)).astype(BF16)
    return hi, mid, lo


def _dot_sel(sel, x):
    hi, mid, lo = _split3(x)
    d = lambda p: jnp.dot(sel, p, preferred_element_type=F32)
    return d(hi) + d(mid) + d(lo)


def _seg_sum(x, ones_bd):
    hi = x.astype(BF16)
    lo = (x - hi.astype(F32)).astype(BF16)
    w = ones_bd.shape[0]
    d = lambda p: jnp.concatenate(
        [jnp.dot(p[:, i:i + w], ones_bd, preferred_element_type=F32) for i in range(0, x.shape[1], w)],
        axis=1)
    return d(hi) + d(lo)


def _div_pow2(x, d):
    assert d & (d - 1) == 0
    return jnp.right_shift(x, d.bit_length() - 1)


def _rms(x, g):
    return x * lax.rsqrt(jnp.mean(x * x, axis=-1, keepdims=True) + RMS_EPS) * g


def _sigmoid(x):
    return 1.0 / (1.0 + jnp.exp(-x))


def _softplus(x):
    return jnp.maximum(x, 0.0) + jnp.log(1.0 + jnp.exp(-jnp.abs(x)))


def _prep_kernel(x_ref, shift_ref, pool_ref, *rest, nb, tt, pos0, n_prev):
    rest = list(rest)
    pool_prev_ref = rest.pop(0) if n_prev else None
    (n1_ref, win_ref, mu_ref, dec0_ref, wdec_ref, a0_ref, wa_ref, wg_ref, kk_ref, ka_ref, ones_ref,
     wpool_ref, pscale_ref, wbup_ref,
     r_ref, lw_ref, k_ref, v_ref, a_ref, b_ref, g_ref, ga_ref, gbb_ref,
     shift_out_ref, pool_out_ref, cz_ref, cu_ref) = rest
    if n_prev:
        pool_out_ref[0:n_prev] = pool_prev_ref[...]
    tm = tt * nb
    cz = cz_ref.shape[0]
    cu = cu_ref.shape[0]
    t = pl.program_id(1)
    hist = cu // nb if nb == 1 else pool_ref.shape[0]

    def put(ref, val):
        if nb == 1:
            ref[...] = val.reshape(ref.shape)
        else:
            for i in range(tt):
                ref[:, i, :] = val[i * nb:(i + 1) * nb]

    @pl.when(t == 0)
    def _():
        cz_ref[...] = shift_ref[...].reshape(cz_ref.shape)
        past = pool_ref[...].reshape(hist * nb, D_POOL)
        if hist * nb < cu:
            past = jnp.concatenate([jnp.zeros((cu - hist * nb, D_POOL), F32), past], axis=0)
        cu_ref[...] = past

    if nb == 1:
        x = x_ref[...].reshape(tm, D_MODEL)
    else:
        x = jnp.concatenate([x_ref[:, i, :] for i in range(tt)], axis=0)
    hb = _rms(x, n1_ref[...]).astype(BF16)

    c0 = D_SHIFT + D_POOL
    z_rw = jnp.dot(hb, win_ref[:, 0:D_SHIFT], preferred_element_type=F32)
    u = jnp.dot(hb, win_ref[:, D_SHIFT:c0], preferred_element_type=F32)
    z_ga = jnp.dot(hb, win_ref[:, c0:c0 + D_MODEL], preferred_element_type=F32)
    z_gb = jnp.dot(hb, win_ref[:, c0 + D_MODEL:c0 + 2 * D_MODEL], preferred_element_type=F32)

    ext = jnp.concatenate([cz_ref[...], z_rw], axis=0)
    z_prev = pltpu.roll(ext, nb, 0)[cz:]
    zs = z_rw + (z_prev - z_rw) * mu_ref[...]
    new_cz = z_rw[tm - cz:]
    cz_ref[...] = new_cz
    shift_out_ref[...] = new_cz.reshape(shift_out_ref.shape)

    r = zs[:, 0:D_A]
    k = zs[:, D_A:2 * D_A]
    v = zs[:, 2 * D_A:3 * D_A]
    lwa = zs[:, 3 * D_A:3 * D_A + LORA_WA]
    lg = zs[:, 3 * D_A + LORA_WA:D_SHIFT]
    w_log = -_softplus(-(dec0_ref[...] + _dot(jnp.tanh(lwa), wdec_ref[...]))) - 0.5
    a_in = _sigmoid(a0_ref[...] + _dot(lwa, wa_ref[...]))
    kk = k * kk_ref[...]
    kk = kk / jnp.maximum(jnp.sqrt(_seg_sum(kk * kk, ones_ref[...])), L2_EPS)
    put(r_ref, r)
    put(lw_ref, -jnp.exp(w_log))
    put(k_ref, k * (1.0 + (a_in - 1.0) * ka_ref[...]))
    put(v_ref, v)
    put(a_ref, -kk)
    put(b_ref, kk * a_in)
    put(g_ref, _dot(_sigmoid(lg), wg_ref[...]))

    extu = jnp.concatenate([cu_ref[...], u], axis=0)
    new_cu = extu[tm:]
    cu_ref[...] = new_cu
    pool_out_ref[n_prev] = new_cu[cu - hist * nb:].reshape(pool_out_ref.shape[1:])
    row = lax.broadcasted_iota(jnp.int32, (tm, 1), 0)
    step = _div_pow2(row, nb)
    pos = pos0 + t * tt + step
    parts = []
    for gi, win in enumerate(POOL_WINDOWS):
        sl = slice(gi * POOL_GD, (gi + 1) * POOL_GD)
        s = extu[:, sl]
        span = 1
        while span < win:
            s = s + pltpu.roll(s, span * nb, 0)
            span *= 2
        cnt = jnp.minimum(pos + 1, win).astype(F32)
        p = s[cu:] / cnt - u[:, sl]
        parts.append(_dot(p, wpool_ref[gi]))
    pooled = jnp.concatenate(parts, axis=-1) * pscale_ref[...]
    b_out = _dot(pooled, wbup_ref[...])

    put(ga_ref, _sigmoid(z_ga))
    put(gbb_ref, _sigmoid(z_gb) * b_out)


def _layer_spec(w, l):
    li = min(l, w.shape[0] - 1)
    nd = w.ndim - 1
    return pl.BlockSpec((None,) + w.shape[1:], lambda *_: (li,) + (0,) * nd, pipeline_mode=pl.Buffered(1))


def _prep_call(x, shift_in, pool_in, pool_prev, wts, l, *, nb, tt, pos0, grid, row_block, row_map,
               shift_shape, shift_in_map, shift_out_map, pool_shape, pool_block, pool_in_spec, pool_stack_map):
    lead = x.shape[:-1]
    n_prev = 0 if pool_prev is None else pool_prev.shape[0]
    outs_rows = lambda w: jax.ShapeDtypeStruct(lead + (w,), F32)
    rb = lambda w: pl.BlockSpec(row_block + (w,), row_map)
    cz = max(SUBLANES, nb)
    cu = POOL_CARRY * nb
    shift_block = (1, cz, D_SHIFT)
    stacked = lambda n: pl.BlockSpec((n,) + pool_block, pool_stack_map)
    in_specs = ([rb(D_MODEL), pl.BlockSpec(shift_block, shift_in_map), pool_in_spec]
                + ([stacked(n_prev)] if n_prev else []) + [_layer_spec(w, l) for w in wts])
    out_shape = [outs_rows(D_A)] * 7 + [outs_rows(D_MODEL)] * 2 + [
        jax.ShapeDtypeStruct(shift_shape, F32), jax.ShapeDtypeStruct((n_prev + 1,) + pool_shape, F32)]
    out_specs = [rb(D_A)] * 7 + [rb(D_MODEL)] * 2 + [pl.BlockSpec(shift_block, shift_out_map),
                                                      stacked(n_prev + 1)]
    return pl.pallas_call(
        functools.partial(_prep_kernel, nb=nb, tt=tt, pos0=pos0, n_prev=n_prev),
        grid=grid, in_specs=in_specs, out_specs=out_specs, out_shape=out_shape,
        scratch_shapes=[pltpu.VMEM((cz, D_SHIFT), F32), pltpu.VMEM((cu, D_POOL), F32)],
        compiler_params=pltpu.CompilerParams(dimension_semantics=("arbitrary", "arbitrary"),
                                             vmem_limit_bytes=VMEM_LIMIT),
        name="prep",
    )(x, shift_in, pool_in, *([pool_prev] if n_prev else []), *wts)


def _scan_masks(steps):
    assert CHUNK == HEAD_DIM
    ri = lax.broadcasted_iota(jnp.int32, (CHUNK, LANES), 0)
    ci = jnp.bitwise_and(lax.broadcasted_iota(jnp.int32, (CHUNK, LANES), 1), CHUNK - 1)
    blk = lambda s: _div_pow2(ri, s) == _div_pow2(ci, s)
    same_seq = blk(steps)
    strict = same_seq & (ri > ci)
    incl = same_seq & (ri >= ci)
    levels = []
    size = SUBLANES
    while size < steps:
        size *= 2
        levels.append(strict & blk(size) & ~blk(size // 2))
    base = strict & blk(SUBLANES)
    eye = ri == ci
    return same_seq, strict, incl, base, levels, eye


def _bdot(a, b):
    return lax.dot_general(a.astype(BF16), b.astype(BF16), (((2,), (1,)), ((0,), (0,))),
                           preferred_element_type=F32)


def _bdot_nt(a, b):
    return lax.dot_general(a.astype(BF16), b.astype(BF16), (((2,), (2,)), ((0,), (0,))),
                           preferred_element_type=F32)


def _bdot_tn(a, b):
    return lax.dot_general(a.astype(BF16), b.astype(BF16), (((1,), (1,)), ((0,), (0,))),
                           preferred_element_type=F32)


def _head_stack(x):
    first = lax.broadcasted_iota(jnp.int32, (1, 1, LANES), 2) < HEAD_DIM
    return jnp.concatenate([jnp.where(first, x, 0.0), jnp.where(first, 0.0, x)], axis=1)


def _hdot(a, b):
    return _bdot(a, _head_stack(b))


def _tri_inverse(a_strict, base, levels, eye):
    ad = jnp.where(base, a_strict, 0.0)
    p2 = _hdot(ad, ad)
    t = jnp.where(eye, 1.0, ad)
    t = t + _hdot(t, p2)
    t = t + _hdot(t, _hdot(p2, p2))
    for lvl in levels:
        t = t + _hdot(t, _hdot(jnp.where(lvl, a_strict, 0.0), t))
    return t


def _scan_kernel(r_ref, lw_ref, k_ref, v_ref, a_ref, b_ref, h0_ref, *rest, steps, n_chunks, n_prev):
    y_ref, hout_ref, h_ref = rest[-3:]
    if n_prev:
        hout_ref[0:n_prev] = rest[0][...]
    nseq = CHUNK // steps
    nst = h_ref.shape[0]
    t = pl.program_id(1)

    @pl.when(t == 0)
    def _():
        zero = jnp.zeros((HEAD_DIM, HEAD_DIM), F32)
        for s in range(nst):
            for p in range(PAIRS):
                top = jnp.concatenate([h0_ref[0, s, p, 0], zero], axis=1)
                bot = jnp.concatenate([zero, h0_ref[0, s, p, 1]], axis=1)
                h_ref[s, p] = jnp.concatenate([top, bot], axis=0)

    same_seq, strict, incl, base, levels, eye = _scan_masks(steps)
    sel = jnp.concatenate([incl[:, :CHUNK], same_seq[:, :CHUNK]], axis=0).astype(BF16)
    hi = lax.broadcasted_iota(jnp.int32, (LANES, LANES), 0)
    hj = lax.broadcasted_iota(jnp.int32, (LANES, LANES), 1)
    same_head = _div_pow2(hi, HEAD_DIM) == _div_pow2(hj, HEAD_DIM)
    rows = n_chunks * CHUNK

    ld = lambda ref: ref[...].reshape(rows, D_A)
    r, lw, k, v, a, b = (ld(x) for x in (r_ref, lw_ref, k_ref, v_ref, a_ref, b_ref))
    cc = [_dot_sel(sel, lw[c * CHUNK:(c + 1) * CHUNK]) for c in range(n_chunks)]
    cs = jnp.concatenate([x[:CHUNK] for x in cc], axis=0)
    cl = jnp.concatenate([x[CHUNK:] for x in cc], axis=0)
    e_cs = jnp.exp(cs)
    e_ncs = jnp.exp(-cs)
    e_cl = jnp.exp(cl - cs)
    gam = jnp.exp(cl)

    def inst(x):
        x3 = x.reshape(n_chunks, CHUNK, D_A)
        return jnp.concatenate([x3[:, :, p * LANES:(p + 1) * LANES] for p in range(PAIRS)], axis=0)

    at = inst(a * jnp.exp(cs - lw))
    rt = inst(r * e_cs)
    bh = inst(b * e_cl)
    kh = inst(k * e_cl)
    vv = inst(v)
    aa = _bdot_nt(jnp.concatenate([at, rt], axis=1),
                  jnp.concatenate([_head_stack(inst(b * e_ncs)), _head_stack(inst(k * e_ncs))], axis=1))
    a_ab = jnp.where(strict, aa[:, :CHUNK, :LANES], 0.0)
    a_ak = jnp.where(strict, aa[:, :CHUNK, LANES:], 0.0)
    a_rb = jnp.where(incl, aa[:, CHUNK:, :LANES], 0.0)
    a_rk = jnp.where(incl, aa[:, CHUNK:, LANES:], 0.0)
    tinv = _tri_inverse(a_ab, base, levels, eye)
    akv = _hdot(jnp.concatenate([a_ak, a_rk], axis=1), vv)
    tx = _bdot(tinv, jnp.concatenate([_head_stack(at), _head_stack(akv[:, :CHUNK])], axis=2))
    ah = tx[:, :, :LANES]
    w1 = tx[:, :, LANES:]
    rx = _bdot(a_rb, jnp.concatenate([_head_stack(ah), _head_stack(w1)], axis=2))
    rh = rt + rx[:, :, :LANES]
    y1 = rx[:, :, LANES:] + akv[:, CHUNK:]

    ng = PAIRS * n_chunks
    bf16_rows = 2 * SUBLANES

    def seqsplit(x):
        parts = [x[:, s * steps:(s + 1) * steps] for s in range(nseq)]
        if steps < bf16_rows:
            pad = jnp.zeros((x.shape[0], bf16_rows - steps, x.shape[2]), F32)
            parts = [jnp.concatenate([q, pad], axis=1) for q in parts]
        return jnp.concatenate(parts, axis=0)

    bhs = seqsplit(bh)
    mp = jnp.where(same_head, _bdot_tn(seqsplit(ah), bhs), 0.0)
    np_ = jnp.where(same_head, _bdot_tn(jnp.concatenate([seqsplit(w1), seqsplit(vv)], axis=1),
                                        jnp.concatenate([bhs, seqsplit(kh)], axis=1)), 0.0)
    gam_i = inst(gam)
    gam_m = jnp.concatenate([gam_i[:, s * steps:s * steps + 1] for s in range(nseq)], axis=0)
    order = [(s, p, c) for s in range(nseq) for p in range(PAIRS) for c in range(n_chunks)]
    n_groups = nst // nseq
    cpg = n_chunks // n_groups
    lanes_of = [(q, s, p) for q in range(n_groups) for s in range(nseq) for p in range(PAIRS)]
    take = lambda x, pos: jnp.concatenate(
        [x[s * ng + p * n_chunks + q * cpg + pos][None] for (q, s, p) in lanes_of], axis=0)
    cur = jnp.concatenate([h_ref[q * nseq + s, p][None] for (q, s, p) in lanes_of], axis=0)
    starts = []
    for pos in range(cpg):
        starts.append(cur)
        cur = cur * take(gam_m, pos) + _bdot(cur, take(mp, pos)) + take(np_, pos)
    slot = {key: i for i, key in enumerate(lanes_of)}
    s_start = jnp.concatenate(
        [starts[c % cpg][slot[(c // cpg, s, p)]][None] for (s, p, c) in order], axis=0)
    finals = {(q * nseq + s, p): cur[i] for i, (q, s, p) in enumerate(lanes_of)}
    y = (_bdot_nt(seqsplit(rh), s_start) + seqsplit(y1))[:, :steps]
    for i, (s, p, c) in enumerate(order):
        r0 = c * CHUNK + s * steps
        if len(y_ref.shape) == 3:
            per = y_ref.shape[1]
            y_ref[r0 // per, r0 % per:r0 % per + steps, p * LANES:(p + 1) * LANES] = y[i]
        else:
            y_ref[r0:r0 + steps, p * LANES:(p + 1) * LANES] = y[i]
    for (s, p), h in finals.items():
        h_ref[s, p] = h
        hout_ref[n_prev, 0, s, p, 0] = h[:HEAD_DIM, :HEAD_DIM]
        hout_ref[n_prev, 0, s, p, 1] = h[HEAD_DIM:, HEAD_DIM:]


def _scan_call(vecs, h0, prev, l, *, steps, n_chunks, grid, row_block, row_map):
    nst = h0.shape[2]
    li = min(l, h0.shape[0] - 1)
    n_prev = 0 if prev is None else prev.shape[0]
    st_block = (1, nst, PAIRS, 2, HEAD_DIM, HEAD_DIM)
    rb = pl.BlockSpec(row_block + (D_A,), row_map)
    hb = pl.BlockSpec((None,) + st_block, lambda b, t: (li, b, 0, 0, 0, 0, 0))
    stacked = lambda n: pl.BlockSpec((n,) + st_block, lambda b, t: (0, b, 0, 0, 0, 0, 0))
    return pl.pallas_call(
        functools.partial(_scan_kernel, steps=steps, n_chunks=n_chunks, n_prev=n_prev),
        grid=grid, in_specs=[rb] * 6 + [hb] + ([stacked(n_prev)] if n_prev else []),
        out_specs=[rb, stacked(n_prev + 1)],
        out_shape=[jax.ShapeDtypeStruct(vecs[0].shape, F32),
                   jax.ShapeDtypeStruct((n_prev + 1,) + h0.shape[1:], F32)],
        scratch_shapes=[pltpu.VMEM((nst, PAIRS, LANES, LANES), F32)],
        compiler_params=pltpu.CompilerParams(dimension_semantics=("arbitrary", "arbitrary"),
                                             vmem_limit_bytes=VMEM_LIMIT),
        name="scan",
    )(*vecs, h0, *([prev] if n_prev else []))


def _post_kernel(y_ref, r_ref, k_ref, v_ref, g_ref, ga_ref, gbb_ref, x_ref, ones_ref, rk_ref, lng_ref,
                 lnb_ref, waup_ref, wo_ref, n2_ref, w1_ref, w2_ref, fn_ref, o_ref, *, final):
    ones = ones_ref[...]
    y = y_ref[...]
    mu = _seg_sum(y, ones) * (1.0 / HEAD_DIM)
    d = y - mu
    var = _seg_sum(d * d, ones) * (1.0 / HEAD_DIM)
    yn = d * lax.rsqrt(var + GN_EPS) * lng_ref[...] + lnb_ref[...]
    bonus = _seg_sum(r_ref[...] * k_ref[...] * rk_ref[...], ones) * v_ref[...]
    a_out = _dot((yn + bonus) * g_ref[...], waup_ref[...])
    merged = ga_ref[...] * a_out + gbb_ref[...]
    x1 = x_ref[...] + _dot(merged, wo_ref[...])
    hb = _rms(x1, n2_ref[...]).astype(BF16)
    x2 = x1
    for c in range(0, D_FF, FF_CHUNK):
        f = jnp.maximum(jnp.dot(hb, w1_ref[:, c:c + FF_CHUNK], preferred_element_type=F32), 0.0)
        x2 = x2 + _dot(f * f, w2_ref[c:c + FF_CHUNK, :])
    o_ref[...] = _rms(x2, fn_ref[...]) if final else x2


def _post_call(rows_in, wts, l, *, final, tm):
    n = rows_in[0].shape[0]
    rb = lambda a: pl.BlockSpec((tm, a.shape[-1]), lambda i: (i, 0))
    return pl.pallas_call(
        functools.partial(_post_kernel, final=final),
        grid=(n // tm,),
        in_specs=[rb(a) for a in rows_in] + [_layer_spec(w, l) for w in wts],
        out_specs=pl.BlockSpec((tm, D_MODEL), lambda i: (i, 0)),
        out_shape=jax.ShapeDtypeStruct((n, D_MODEL), F32),
        compiler_params=pltpu.CompilerParams(dimension_semantics=("arbitrary",),
                                             vmem_limit_bytes=VMEM_LIMIT),
        name="post",
    )(*rows_in, *wts)


def _stacked_weights(w_in, mu_shift, decay0, w_decay2, a0, w_a2, w_g2, k_k, k_a, r_k, ln_x_g, ln_x_b,
                     w_a_up, w_pool, pool_scale, w_b_up, w_o, norm1_g, norm2_g, w_ff1, w_ff2, final_norm_g):
    depth = w_in.shape[0]
    row = lambda a: a.reshape(a.shape[0], 1, -1)
    zpad = jnp.zeros((depth, LORA_WA // 2, D_A), F32)
    seg = jnp.arange(MXU_DIM) // HEAD_DIM
    ones_bd = (seg[:, None] == seg[None, :]).astype(BF16)[None]
    prep = [row(norm1_g), w_in.astype(BF16), row(mu_shift), row(decay0),
            jnp.concatenate([w_decay2, zpad], axis=1).astype(BF16), row(a0),
            jnp.concatenate([zpad, w_a2], axis=1).astype(BF16), w_g2.astype(BF16),
            row(k_k), row(k_a), ones_bd, w_pool.astype(BF16), row(pool_scale), w_b_up.astype(BF16)]
    post = [ones_bd, row(r_k), row(ln_x_g), row(ln_x_b), w_a_up.astype(BF16), w_o.astype(BF16),
            row(norm2_g), w_ff1.astype(BF16), w_ff2.astype(BF16), final_norm_g.reshape(1, 1, -1)]
    return prep, post


def kernel(x_prompt, x_sample, state_shift, state_pool, state_wkv, norm1_g, w_in, mu_shift, decay0,
           w_decay2, a0, w_a2, w_g2, k_k, k_a, r_k, ln_x_g, ln_x_b, w_a_up, w_pool, pool_scale, w_b_up,
           w_o, norm2_g, w_ff1, w_ff2, final_norm_g):
    depth = w_in.shape[0]
    bp, tp, _ = x_prompt.shape
    bs, ts, _ = x_sample.shape
    tt_p = 256
    nb_s = 32
    ts_p = 256
    nb_scan = 2
    tm_post = 512
    nc_s = 2
    nseq = CHUNK // ts
    hist = state_pool.shape[2]
    prep_w, post_w = _stacked_weights(
        w_in, mu_shift, decay0, w_decay2, a0, w_a2, w_g2, k_k, k_a, r_k, ln_x_g, ln_x_b, w_a_up, w_pool,
        pool_scale, w_b_up, w_o, norm1_g, norm2_g, w_ff1, w_ff2, final_norm_g)

    p_shift0 = jnp.zeros((bp, SUBLANES, D_SHIFT), F32)
    p_pool0 = jnp.zeros((bp, POOL_CARRY, D_POOL), F32)
    p_wkv0 = jnp.zeros((1, bp // nb_scan, nb_scan, PAIRS, 2, HEAD_DIM, HEAD_DIM), F32)
    s_pool0 = jnp.swapaxes(state_pool, 1, 2)
    s_wkv0 = state_wkv.reshape(depth, bs // (nseq * nc_s), nseq * nc_s, PAIRS, 2, HEAD_DIM, HEAD_DIM)

    xp = x_prompt
    xs = x_sample.reshape(bs * ts, D_MODEL)
    outs = {k: [] for k in ("p_shift", "s_shift")}
    p_pool = p_wkv = s_pool = s_wkv = None
    for l in range(depth):
        final = l == depth - 1

        res = _prep_call(
            xp, p_shift0, p_pool0, p_pool, prep_w, l, nb=1, tt=tt_p, pos0=0, grid=(bp, tp // tt_p),
            row_block=(1, tt_p), row_map=lambda b, t: (b, t, 0),
            shift_shape=p_shift0.shape, shift_in_map=lambda b, t: (b, 0, 0),
            shift_out_map=lambda b, t: (b, 0, 0), pool_shape=p_pool0.shape,
            pool_block=(1, POOL_CARRY, D_POOL),
            pool_in_spec=pl.BlockSpec((1, POOL_CARRY, D_POOL), lambda b, t: (b, 0, 0)),
            pool_stack_map=lambda b, t: (0, b, 0, 0))
        r, lw, k, v, a, b, g, ga, gbb, sh, p_pool = res
        outs["p_shift"].append(sh[:, SUBLANES - 1])
        y, p_wkv = _scan_call(
            (r, lw, k, v, a, b), p_wkv0, p_wkv, l, steps=CHUNK, n_chunks=nb_scan * ts_p // CHUNK,
            grid=(bp // nb_scan, tp // ts_p), row_block=(nb_scan, ts_p), row_map=lambda b, t: (b, t, 0))
        flat = lambda z: z.reshape(bp * tp, z.shape[-1])
        xp = _post_call([flat(z) for z in (y, r, k, v, g, ga, gbb, xp)], post_w, l, final=final,
                        tm=tm_post).reshape(bp, tp, D_MODEL)

        res = _prep_call(
            xs.reshape(bs, ts, D_MODEL), state_shift, s_pool0, s_pool, prep_w, l, nb=nb_s, tt=ts,
            pos0=PAST_LEN, grid=(bs // nb_s, 1), row_block=(nb_s, ts), row_map=lambda b, t: (b, 0, 0),
            shift_shape=(1, bs, D_SHIFT), shift_in_map=lambda b, t, l=l: (l, b, 0),
            shift_out_map=lambda b, t: (0, b, 0), pool_shape=(hist, bs, D_POOL),
            pool_block=(hist, nb_s, D_POOL),
            pool_in_spec=pl.BlockSpec((None, hist, nb_s, D_POOL), lambda b, t, l=l: (l, 0, b, 0)),
            pool_stack_map=lambda b, t: (0, 0, b, 0))
        r, lw, k, v, a, b, g, ga, gbb = (z.reshape(bs * ts, z.shape[-1]) for z in res[:9])
        outs["s_shift"].append(res[9][0])
        s_pool = res[10]
        y, s_wkv = _scan_call(
            (r, lw, k, v, a, b), s_wkv0, s_wkv, l, steps=ts, n_chunks=nc_s,
            grid=(bs // (nseq * nc_s), 1), row_block=(nc_s * CHUNK,), row_map=lambda b, t: (b, 0))
        xs = _post_call([y, r, k, v, g, ga, gbb, xs], post_w, l, final=final, tm=tm_post)

    st = lambda key: jnp.stack(outs[key])
    wkv = lambda h, n: h.reshape(depth, n, N_HEADS, HEAD_DIM, HEAD_DIM)
    return (xp, xs.reshape(bs, ts, D_MODEL), st("p_shift"), p_pool[:, :, 1:], wkv(p_wkv, bp),
            st("s_shift"), jnp.swapaxes(s_pool, 1, 2), wkv(s_wkv, bs))
```

```python
import functools

import jax
import jax.numpy as jnp
from jax import lax
from jax.experimental import pallas as pl
from jax.experimental.pallas import tpu as pltpu

F32 = jnp.float32
BF16 = jnp.bfloat16

D_MODEL = 1024
HEAD_DIM = 64
D_A = 512
N_HEADS = 8
LORA_WA = 128
LORA_G = 128
D_SHIFT = 3 * D_A + LORA_WA + LORA_G
D_POOL = 512
POOL_WINDOWS = (2, 4, 8, 16)
POOL_GD = 128
POOL_CARRY = 16
D_FF = 4096
PAST_LEN = 16384
RMS_EPS = 1e-6
GN_EPS = HEAD_DIM * 1e-5
L2_EPS = 1e-12

LANES = 128
SUBLANES = 8
MXU_DIM = 256
CHUNK = 64
FF_CHUNK = 1024
PAIRS = D_A // LANES
VMEM_LIMIT = 56 * 1024 * 1024


def _dot(a, b):
    return jnp.dot(a.astype(BF16), b.astype(BF16), preferred_element_type=F32)


def _split3(x):
    hi = x.astype(BF16)
    r1 = x - hi.astype(F32)
    mid = r1.astype(BF16)
    lo = (r1 - mid.astype(F32)).astype(BF16)
    return hi, mid, lo


def _dot_sel(sel, x):
    hi, mid, lo = _split3(x)
    d = lambda p: jnp.dot(sel, p, preferred_element_type=F32)
    return d(hi) + d(mid) + d(lo)


def _seg_sum(x, ones_bd):
    hi = x.astype(BF16)
    lo = (x - hi.astype(F32)).astype(BF16)
    w = ones_bd.shape[0]
    d = lambda p: jnp.concatenate(
        [jnp.dot(p[:, i:i + w], ones_bd, preferred_element_type=F32) for i in range(0, x.shape[1], w)],
        axis=1)
    return d(hi) + d(lo)


def _div_pow2(x, d):
    assert d & (d - 1) == 0
    return jnp.right_shift(x, d.bit_length() - 1)


def _rms(x, g):
    return x * lax.rsqrt(jnp.mean(x * x, axis=-1, keepdims=True) + RMS_EPS) * g


def _sigmoid(x):
    return 1.0 / (1.0 + jnp.exp(-x))


def _softplus(x):
    return jnp.maximum(x, 0.0) + jnp.log(1.0 + jnp.exp(-jnp.abs(x)))


def _prep_kernel(x_ref, shift_ref, pool_ref, *rest, nb, tt, pos0, n_prev):
    rest = list(rest)
    pool_prev_ref = rest.pop(0) if n_prev else None
    (n1_ref, win_ref, mu_ref, dec0_ref, wdec_ref, a0_ref, wa_ref, wg_ref, kk_ref, ka_ref, ones_ref,
     wpool_ref, pscale_ref, wbup_ref,
     r_ref, lw_ref, k_ref, v_ref, a_ref, b_ref, g_ref, ga_ref, gbb_ref,
     shift_out_ref, pool_out_ref, cz_ref, cu_ref) = rest
    if n_prev:
        pool_out_ref[0:n_prev] = pool_prev_ref[...]
    tm = tt * nb
    cz = cz_ref.shape[0]
    cu = cu_ref.shape[0]
    t = pl.program_id(1)
    hist = cu // nb if nb == 1 else pool_ref.shape[0]

    def put(ref, val):
        if nb == 1:
            ref[...] = val.reshape(ref.shape)
        else:
            for i in range(tt):
                ref[:, i, :] = val[i * nb:(i + 1) * nb]

    @pl.when(t == 0)
    def _():
        cz_ref[...] = shift_ref[...].reshape(cz_ref.shape)
        past = pool_ref[...].reshape(hist * nb, D_POOL)
        if hist * nb < cu:
            past = jnp.concatenate([jnp.zeros((cu - hist * nb, D_POOL), F32), past], axis=0)
        cu_ref[...] = past

    if nb == 1:
        x = x_ref[...].reshape(tm, D_MODEL)
    else:
        x = jnp.concatenate([x_ref[:, i, :] for i in range(tt)], axis=0)
    hb = _rms(x, n1_ref[...]).astype(BF16)

    c0 = D_SHIFT + D_POOL
    z_rw = jnp.dot(hb, win_ref[:, 0:D_SHIFT], preferred_element_type=F32)
    u = jnp.dot(hb, win_ref[:, D_SHIFT:c0], preferred_element_type=F32)
    z_ga = jnp.dot(hb, win_ref[:, c0:c0 + D_MODEL], preferred_element_type=F32)
    z_gb = jnp.dot(hb, win_ref[:, c0 + D_MODEL:c0 + 2 * D_MODEL], preferred_element_type=F32)

    ext = jnp.concatenate([cz_ref[...], z_rw], axis=0)
    z_prev = pltpu.roll(ext, nb, 0)[cz:]
    zs = z_rw + (z_prev - z_rw) * mu_ref[...]
    new_cz = z_rw[tm - cz:]
    cz_ref[...] = new_cz
    shift_out_ref[...] = new_cz.reshape(shift_out_ref.shape)

    r = zs[:, 0:D_A]
    k = zs[:, D_A:2 * D_A]
    v = zs[:, 2 * D_A:3 * D_A]
    lwa = zs[:, 3 * D_A:3 * D_A + LORA_WA]
    lg = zs[:, 3 * D_A + LORA_WA:D_SHIFT]
    w_log = -_softplus(-(dec0_ref[...] + _dot(jnp.tanh(lwa), wdec_ref[...]))) - 0.5
    a_in = _sigmoid(a0_ref[...] + _dot(lwa, wa_ref[...]))
    kk = k * kk_ref[...]
    kk = kk / jnp.maximum(jnp.sqrt(_seg_sum(kk * kk, ones_ref[...])), L2_EPS)
    put(r_ref, r)
    put(lw_ref, -jnp.exp(w_log))
    put(k_ref, k * (1.0 + (a_in - 1.0) * ka_ref[...]))
    put(v_ref, v)
    put(a_ref, -kk)
    put(b_ref, kk * a_in)
    put(g_ref, _dot(_sigmoid(lg), wg_ref[...]))

    extu = jnp.concatenate([cu_ref[...], u], axis=0)
    new_cu = extu[tm:]
    cu_ref[...] = new_cu
    pool_out_ref[n_prev] = new_cu[cu - hist * nb:].reshape(pool_out_ref.shape[1:])
    row = lax.broadcasted_iota(jnp.int32, (tm, 1), 0)
    step = _div_pow2(row, nb)
    pos = pos0 + t * tt + step
    parts = []
    for gi, win in enumerate(POOL_WINDOWS):
        sl = slice(gi * POOL_GD, (gi + 1) * POOL_GD)
        s = extu[:, sl]
        span = 1
        while span < win:
            s = s + pltpu.roll(s, span * nb, 0)
            span *= 2
        cnt = jnp.minimum(pos + 1, win).astype(F32)
        p = s[cu:] / cnt - u[:, sl]
        parts.append(_dot(p, wpool_ref[gi]))
    pooled = jnp.concatenate(parts, axis=-1) * pscale_ref[...]
    b_out = _dot(pooled, wbup_ref[...])

    put(ga_ref, _sigmoid(z_ga))
    put(gbb_ref, _sigmoid(z_gb) * b_out)


def _layer_spec(w, l):
    li = min(l, w.shape[0] - 1)
    nd = w.ndim - 1
    return pl.BlockSpec((None,) + w.shape[1:], lambda *_: (li,) + (0,) * nd, pipeline_mode=pl.Buffered(1))


def _prep_call(x, shift_in, pool_in, pool_prev, wts, l, *, nb, tt, pos0, grid, row_block, row_map,
               shift_shape, shift_in_map, shift_out_map, pool_shape, pool_block, pool_in_spec, pool_stack_map):
    lead = x.shape[:-1]
    n_prev = 0 if pool_prev is None else pool_prev.shape[0]
    outs_rows = lambda w: jax.ShapeDtypeStruct(lead + (w,), F32)
    rb = lambda w: pl.BlockSpec(row_block + (w,), row_map)
    cz = max(SUBLANES, nb)
    cu = POOL_CARRY * nb
    shift_block = (1, cz, D_SHIFT)
    stacked = lambda n: pl.BlockSpec((n,) + pool_block, pool_stack_map)
    in_specs = ([rb(D_MODEL), pl.BlockSpec(shift_block, shift_in_map), pool_in_spec]
                + ([stacked(n_prev)] if n_prev else []) + [_layer_spec(w, l) for w in wts])
    out_shape = [outs_rows(D_A)] * 7 + [outs_rows(D_MODEL)] * 2 + [
        jax.ShapeDtypeStruct(shift_shape, F32), jax.ShapeDtypeStruct((n_prev + 1,) + pool_shape, F32)]
    out_specs = [rb(D_A)] * 7 + [rb(D_MODEL)] * 2 + [pl.BlockSpec(shift_block, shift_out_map),
                                                      stacked(n_prev + 1)]
    return pl.pallas_call(
        functools.partial(_prep_kernel, nb=nb, tt=tt, pos0=pos0, n_prev=n_prev),
        grid=grid, in_specs=in_specs, out_specs=out_specs, out_shape=out_shape,
        scratch_shapes=[pltpu.VMEM((cz, D_SHIFT), F32), pltpu.VMEM((cu, D_POOL), F32)],
        compiler_params=pltpu.CompilerParams(dimension_semantics=("arbitrary", "arbitrary"),
                                             vmem_limit_bytes=VMEM_LIMIT),
        name="prep",
    )(x, shift_in, pool_in, *([pool_prev] if n_prev else []), *wts)


def _scan_masks(steps):
    assert CHUNK == HEAD_DIM
    ri = lax.broadcasted_iota(jnp.int32, (CHUNK, LANES), 0)
    ci = jnp.bitwise_and(lax.broadcasted_iota(jnp.int32, (CHUNK, LANES), 1), CHUNK - 1)
    blk = lambda s: _div_pow2(ri, s) == _div_pow2(ci, s)
    same_seq = blk(steps)
    strict = same_seq & (ri > ci)
    incl = same_seq & (ri >= ci)
    levels = []
    size = SUBLANES
    while size < steps:
        size *= 2
        levels.append(strict & blk(size) & ~blk(size // 2))
    base = strict & blk(SUBLANES)
    eye = ri == ci
    return same_seq, strict, incl, base, levels, eye


def _bdot(a, b):
    return lax.dot_general(a.astype(BF16), b.astype(BF16), (((2,), (1,)), ((0,), (0,))),
                           preferred_element_type=F32)


def _bdot_nt(a, b):
    return lax.dot_general(a.astype(BF16), b.astype(BF16), (((2,), (2,)), ((0,), (0,))),
                           preferred_element_type=F32)


def _bdot_tn(a, b):
    return lax.dot_general(a.astype(BF16), b.astype(BF16), (((1,), (1,)), ((0,), (0,))),
                           preferred_element_type=F32)


def _head_stack(x):
    first = lax.broadcasted_iota(jnp.int32, (1, 1, LANES), 2) < HEAD_DIM
    return jnp.concatenate([jnp.where(first, x, 0.0), jnp.where(first, 0.0, x)], axis=1)


def _hdot(a, b):
    return _bdot(a, _head_stack(b))


def _tri_inverse(a_strict, base, levels, eye):
    ad = jnp.where(base, a_strict, 0.0)
    p2 = _hdot(ad, ad)
    t = jnp.where(eye, 1.0, ad)
    t = t + _hdot(t, p2)
    t = t + _hdot(t, _hdot(p2, p2))
    for lvl in levels:
        t = t + _hdot(t, _hdot(jnp.where(lvl, a_strict, 0.0), t))
    return t


def _scan_kernel(r_ref, lw_ref, k_ref, v_ref, a_ref, b_ref, h0_ref, *rest, steps, n_chunks, n_prev):
    y_ref, hout_ref, h_ref = rest[-3:]
    if n_prev:
        hout_ref[0:n_prev] = rest[0][...]
    nseq = CHUNK // steps
    nst = h_ref.shape[0]
    t = pl.program_id(1)

    @pl.when(t == 0)
    def _():
        h_ref[...] = _head_stack(h0_ref[0].reshape(nst * PAIRS, HEAD_DIM, LANES)).reshape(h_ref.shape)

    same_seq, strict, incl, base, levels, eye = _scan_masks(steps)
    sel = jnp.concatenate([incl[:, :CHUNK], same_seq[:, :CHUNK]], axis=0).astype(BF16)
    hi = lax.broadcasted_iota(jnp.int32, (LANES, LANES), 0)
    hj = lax.broadcasted_iota(jnp.int32, (LANES, LANES), 1)
    same_head = _div_pow2(hi, HEAD_DIM) == _div_pow2(hj, HEAD_DIM)
    rows = n_chunks * CHUNK

    ld = lambda ref: ref[...].reshape(rows, D_A)
    r, lw, k, v, a, b = (ld(x) for x in (r_ref, lw_ref, k_ref, v_ref, a_ref, b_ref))
    cc = [_dot_sel(sel, lw[c * CHUNK:(c + 1) * CHUNK]) for c in range(n_chunks)]
    cs = jnp.concatenate([x[:CHUNK] for x in cc], axis=0)
    cl = jnp.concatenate([x[CHUNK:] for x in cc], axis=0)
    e_cs = jnp.exp(cs)
    e_ncs = jnp.exp(-cs)
    e_cl = jnp.exp(cl - cs)
    gam = jnp.exp(cl)

    def inst(x):
        x3 = x.reshape(n_chunks, CHUNK, D_A)
        return jnp.concatenate([x3[:, :, p * LANES:(p + 1) * LANES] for p in range(PAIRS)], axis=0)

    at = inst(a * jnp.exp(cs - lw))
    rt = inst(r * e_cs)
    bh = inst(b * e_cl)
    kh = inst(k * e_cl)
    vv = inst(v)
    aa = _bdot_nt(jnp.concatenate([at, rt], axis=1),
                  jnp.concatenate([_head_stack(inst(b * e_ncs)), _head_stack(inst(k * e_ncs))], axis=1))
    a_ab = jnp.where(strict, aa[:, :CHUNK, :LANES], 0.0)
    a_ak = jnp.where(strict, aa[:, :CHUNK, LANES:], 0.0)
    a_rb = jnp.where(incl, aa[:, CHUNK:, :LANES], 0.0)
    a_rk = jnp.where(incl, aa[:, CHUNK:, LANES:], 0.0)
    tinv = _tri_inverse(a_ab, base, levels, eye)
    akv = _hdot(jnp.concatenate([a_ak, a_rk], axis=1), vv)
    tx = _bdot(tinv, jnp.concatenate([_head_stack(at), _head_stack(akv[:, :CHUNK])], axis=2))
    ah = tx[:, :, :LANES]
    w1 = tx[:, :, LANES:]
    rx = _bdot(a_rb, jnp.concatenate([_head_stack(ah), _head_stack(w1)], axis=2))
    rh = rt + rx[:, :, :LANES]
    y1 = rx[:, :, LANES:] + akv[:, CHUNK:]

    ng = PAIRS * n_chunks
    bf16_rows = 2 * SUBLANES

    def seqsplit(x):
        parts = [x[:, s * steps:(s + 1) * steps] for s in range(nseq)]
        if steps < bf16_rows:
            pad = jnp.zeros((x.shape[0], bf16_rows - steps, x.shape[2]), F32)
            parts = [jnp.concatenate([q, pad], axis=1) for q in parts]
        return jnp.concatenate(parts, axis=0)

    bhs = seqsplit(bh)
    mp = jnp.where(same_head, _bdot_tn(seqsplit(ah), bhs), 0.0)
    np_ = jnp.where(same_head, _bdot_tn(jnp.concatenate([seqsplit(w1), seqsplit(vv)], axis=1),
                                        jnp.concatenate([bhs, seqsplit(kh)], axis=1)), 0.0)
    gam_i = inst(gam)
    gam_m = jnp.concatenate([gam_i[:, s * steps:s * steps + 1] for s in range(nseq)], axis=0)
    order = [(s, p, c) for s in range(nseq) for p in range(PAIRS) for c in range(n_chunks)]
    n_groups = nst // nseq
    cpg = n_chunks // n_groups
    lanes_of = [(q, s, p) for q in range(n_groups) for s in range(nseq) for p in range(PAIRS)]
    take = lambda x, pos: jnp.concatenate(
        [x[s * ng + p * n_chunks + q * cpg + pos][None] for (q, s, p) in lanes_of], axis=0)
    cur = jnp.concatenate([h_ref[q * nseq + s, p][None] for (q, s, p) in lanes_of], axis=0)
    starts = []
    for pos in range(cpg):
        starts.append(cur)
        cur = cur * take(gam_m, pos) + _bdot(cur, take(mp, pos)) + take(np_, pos)
    slot = {key: i for i, key in enumerate(lanes_of)}
    s_start = jnp.concatenate(
        [starts[c % cpg][slot[(c // cpg, s, p)]][None] for (s, p, c) in order], axis=0)
    finals = {(q * nseq + s, p): cur[i] for i, (q, s, p) in enumerate(lanes_of)}
    y = (_bdot_nt(seqsplit(rh), s_start) + seqsplit(y1))[:, :steps]
    for i, (s, p, c) in enumerate(order):
        r0 = c * CHUNK + s * steps
        if len(y_ref.shape) == 3:
            per = y_ref.shape[1]
            y_ref[r0 // per, r0 % per:r0 % per + steps, p * LANES:(p + 1) * LANES] = y[i]
        else:
            y_ref[r0:r0 + steps, p * LANES:(p + 1) * LANES] = y[i]
    for (s, p), h in finals.items():
        h_ref[s, p] = h
        hout_ref[n_prev, 0, s, p] = h[:HEAD_DIM] + h[HEAD_DIM:]


def _scan_call(vecs, h0, prev, l, *, steps, n_chunks, grid, row_block, row_map):
    nst = h0.shape[2]
    li = min(l, h0.shape[0] - 1)
    n_prev = 0 if prev is None else prev.shape[0]
    st_block = (1, nst, PAIRS, HEAD_DIM, LANES)
    rb = pl.BlockSpec(row_block + (D_A,), row_map)
    hb = pl.BlockSpec((None,) + st_block, lambda b, t: (li, b, 0, 0, 0, 0))
    stacked = lambda n: pl.BlockSpec((n,) + st_block, lambda b, t: (0, b, 0, 0, 0, 0))
    return pl.pallas_call(
        functools.partial(_scan_kernel, steps=steps, n_chunks=n_chunks, n_prev=n_prev),
        grid=grid, in_specs=[rb] * 6 + [hb] + ([stacked(n_prev)] if n_prev else []),
        out_specs=[rb, stacked(n_prev + 1)],
        out_shape=[jax.ShapeDtypeStruct(vecs[0].shape, F32),
                   jax.ShapeDtypeStruct((n_prev + 1,) + h0.shape[1:], F32)],
        scratch_shapes=[pltpu.VMEM((nst, PAIRS, LANES, LANES), F32)],
        compiler_params=pltpu.CompilerParams(dimension_semantics=("arbitrary", "arbitrary"),
                                             vmem_limit_bytes=VMEM_LIMIT),
        name="scan",
    )(*vecs, h0, *([prev] if n_prev else []))


def _post_kernel(y_ref, r_ref, k_ref, v_ref, g_ref, ga_ref, gbb_ref, x_ref, ones_ref, rk_ref, lng_ref,
                 lnb_ref, waup_ref, wo_ref, n2_ref, w1_ref, w2_ref, fn_ref, o_ref, *, final):
    ones = ones_ref[...]
    y = y_ref[...]
    mu = _seg_sum(y, ones) * (1.0 / HEAD_DIM)
    d = y - mu
    var = _seg_sum(d * d, ones) * (1.0 / HEAD_DIM)
    yn = d * lax.rsqrt(var + GN_EPS) * lng_ref[...] + lnb_ref[...]
    bonus = _seg_sum(r_ref[...] * k_ref[...] * rk_ref[...], ones) * v_ref[...]
    a_out = _dot((yn + bonus) * g_ref[...], waup_ref[...])
    merged = ga_ref[...] * a_out + gbb_ref[...]
    x1 = x_ref[...] + _dot(merged, wo_ref[...])
    hb = _rms(x1, n2_ref[...]).astype(BF16)
    x2 = x1
    for c in range(0, D_FF, FF_CHUNK):
        f = jnp.maximum(jnp.dot(hb, w1_ref[:, c:c + FF_CHUNK], preferred_element_type=F32), 0.0)
        x2 = x2 + _dot(f * f, w2_ref[c:c + FF_CHUNK, :])
    o_ref[...] = _rms(x2, fn_ref[...]) if final else x2


def _post_call(rows_in, wts, l, *, final, tm):
    n = rows_in[0].shape[0]
    rb = lambda a: pl.BlockSpec((tm, a.shape[-1]), lambda i: (i, 0))
    return pl.pallas_call(
        functools.partial(_post_kernel, final=final),
        grid=(n // tm,),
        in_specs=[rb(a) for a in rows_in] + [_layer_spec(w, l) for w in wts],
        out_specs=pl.BlockSpec((tm, D_MODEL), lambda i: (i, 0)),
        out_shape=jax.ShapeDtypeStruct((n, D_MODEL), F32),
        compiler_params=pltpu.CompilerParams(dimension_semantics=("arbitrary",),
                                             vmem_limit_bytes=VMEM_LIMIT),
        name="post",
    )(*rows_in, *wts)


def _stacked_weights(w_in, mu_shift, decay0, w_decay2, a0, w_a2, w_g2, k_k, k_a, r_k, ln_x_g, ln_x_b,
                     w_a_up, w_pool, pool_scale, w_b_up, w_o, norm1_g, norm2_g, w_ff1, w_ff2, final_norm_g):
    depth = w_in.shape[0]
    row = lambda a: a.reshape(a.shape[0], 1, -1)
    zpad = jnp.zeros((depth, LORA_WA // 2, D_A), F32)
    seg = jnp.arange(MXU_DIM) // HEAD_DIM
    ones_bd = (seg[:, None] == seg[None, :]).astype(BF16)[None]
    prep = [row(norm1_g), w_in.astype(BF16), row(mu_shift), row(decay0),
            jnp.concatenate([w_decay2, zpad], axis=1).astype(BF16), row(a0),
            jnp.concatenate([zpad, w_a2], axis=1).astype(BF16), w_g2.astype(BF16),
            row(k_k), row(k_a), ones_bd, w_pool.astype(BF16), row(pool_scale), w_b_up.astype(BF16)]
    post = [ones_bd, row(r_k), row(ln_x_g), row(ln_x_b), w_a_up.astype(BF16), w_o.astype(BF16),
            row(norm2_g), w_ff1.astype(BF16), w_ff2.astype(BF16), final_norm_g.reshape(1, 1, -1)]
    return prep, post


def kernel(x_prompt, x_sample, state_shift, state_pool, state_wkv, norm1_g, w_in, mu_shift, decay0,
           w_decay2, a0, w_a2, w_g2, k_k, k_a, r_k, ln_x_g, ln_x_b, w_a_up, w_pool, pool_scale, w_b_up,
           w_o, norm2_g, w_ff1, w_ff2, final_norm_g):
    depth = w_in.shape[0]
    bp, tp, _ = x_prompt.shape
    bs, ts, _ = x_sample.shape
    tt_p = 256
    nb_s = 32
    ts_p = 256
    nb_scan = 2
    tm_post = 512
    nc_s = 2
    nseq = CHUNK // ts
    hist = state_pool.shape[2]
    prep_w, post_w = _stacked_weights(
        w_in, mu_shift, decay0, w_decay2, a0, w_a2, w_g2, k_k, k_a, r_k, ln_x_g, ln_x_b, w_a_up, w_pool,
        pool_scale, w_b_up, w_o, norm1_g, norm2_g, w_ff1, w_ff2, final_norm_g)

    p_shift0 = jnp.zeros((bp, SUBLANES, D_SHIFT), F32)
    p_pool0 = jnp.zeros((bp, POOL_CARRY, D_POOL), F32)
    p_wkv0 = jnp.zeros((1, bp // nb_scan, nb_scan, PAIRS, HEAD_DIM, LANES), F32)
    s_pool0 = jnp.swapaxes(state_pool, 1, 2)
    s_wkv0 = jnp.swapaxes(state_wkv.reshape(depth, bs, PAIRS, 2, HEAD_DIM, HEAD_DIM), 3, 4).reshape(
        depth, bs // (nseq * nc_s), nseq * nc_s, PAIRS, HEAD_DIM, LANES)

    xp = x_prompt
    xs = x_sample.reshape(bs * ts, D_MODEL)
    outs = {k: [] for k in ("p_shift", "s_shift")}
    p_pool = p_wkv = s_pool = s_wkv = None
    for l in range(depth):
        final = l == depth - 1

        res = _prep_call(
            xp, p_shift0, p_pool0, p_pool, prep_w, l, nb=1, tt=tt_p, pos0=0, grid=(bp, tp // tt_p),
            row_block=(1, tt_p), row_map=lambda b, t: (b, t, 0),
            shift_shape=p_shift0.shape, shift_in_map=lambda b, t: (b, 0, 0),
            shift_out_map=lambda b, t: (b, 0, 0), pool_shape=p_pool0.shape,
            pool_block=(1, POOL_CARRY, D_POOL),
            pool_in_spec=pl.BlockSpec((1, POOL_CARRY, D_POOL), lambda b, t: (b, 0, 0)),
            pool_stack_map=lambda b, t: (0, b, 0, 0))
        r, lw, k, v, a, b, g, ga, gbb, sh, p_pool = res
        outs["p_shift"].append(sh[:, SUBLANES - 1])
        y, p_wkv = _scan_call(
            (r, lw, k, v, a, b), p_wkv0, p_wkv, l, steps=CHUNK, n_chunks=nb_scan * ts_p // CHUNK,
            grid=(bp // nb_scan, tp // ts_p), row_block=(nb_scan, ts_p), row_map=lambda b, t: (b, t, 0))
        flat = lambda z: z.reshape(bp * tp, z.shape[-1])
        xp = _post_call([flat(z) for z in (y, r, k, v, g, ga, gbb, xp)], post_w, l, final=final,
                        tm=tm_post).reshape(bp, tp, D_MODEL)

        res = _prep_call(
            xs.reshape(bs, ts, D_MODEL), state_shift, s_pool0, s_pool, prep_w, l, nb=nb_s, tt=ts,
            pos0=PAST_LEN, grid=(bs // nb_s, 1), row_block=(nb_s, ts), row_map=lambda b, t: (b, 0, 0),
            shift_shape=(1, bs, D_SHIFT), shift_in_map=lambda b, t, l=l: (l, b, 0),
            shift_out_map=lambda b, t: (0, b, 0), pool_shape=(hist, bs, D_POOL),
            pool_block=(hist, nb_s, D_POOL),
            pool_in_spec=pl.BlockSpec((None, hist, nb_s, D_POOL), lambda b, t, l=l: (l, 0, b, 0)),
            pool_stack_map=lambda b, t: (0, 0, b, 0))
        r, lw, k, v, a, b, g, ga, gbb = (z.reshape(bs * ts, z.shape[-1]) for z in res[:9])
        outs["s_shift"].append(res[9][0])
        s_pool = res[10]
        y, s_wkv = _scan_call(
            (r, lw, k, v, a, b), s_wkv0, s_wkv, l, steps=ts, n_chunks=nc_s,
            grid=(bs // (nseq * nc_s), 1), row_block=(nc_s * CHUNK,), row_map=lambda b, t: (b, 0))
        xs = _post_call([y, r, k, v, g, ga, gbb, xs], post_w, l, final=final, tm=tm_post)

    st = lambda key: jnp.stack(outs[key])
    wkv = lambda h, n: jnp.swapaxes(h.reshape(depth, n, PAIRS, HEAD_DIM, 2, HEAD_DIM), 3, 4).reshape(
        depth, n, N_HEADS, HEAD_DIM, HEAD_DIM)
    return (xp, xs.reshape(bs, ts, D_MODEL), st("p_shift"), p_pool[:, :, 1:], wkv(p_wkv, bp),
            st("s_shift"), jnp.swapaxes(s_pool, 1, 2), wkv(s_wkv, bs))
```

```python
import functools

import jax
import jax.numpy as jnp
from jax import lax
from jax.experimental import pallas as pl
from jax.experimental.pallas import tpu as pltpu

F32 = jnp.float32
BF16 = jnp.bfloat16

D_MODEL = 1024
HEAD_DIM = 64
D_A = 512
N_HEADS = 8
LORA_WA = 128
LORA_G = 128
D_SHIFT = 3 * D_A + LORA_WA + LORA_G
D_POOL = 512
POOL_WINDOWS = (2, 4, 8, 16)
POOL_GD = 128
POOL_CARRY = 16
D_FF = 4096
PAST_LEN = 16384
RMS_EPS = 1e-6
GN_EPS = HEAD_DIM * 1e-5
L2_EPS = 1e-12

LANES = 128
SUBLANES = 8
MXU_DIM = 256
CHUNK = 64
FF_CHUNK = 1024
PAIRS = D_A // LANES
VMEM_LIMIT = 56 * 1024 * 1024


def _dot(a, b):
    return jnp.dot(a.astype(BF16), b.astype(BF16), preferred_element_type=F32)


def _split3(x):
    hi = x.astype(BF16)
    r1 = x - hi.astype(F32)
    mid = r1.astype(BF16)
    lo = (r1 - mid.astype(F32)).astype(BF16)
    return hi, mid, lo


def _dot_sel(sel, x):
    hi, mid, lo = _split3(x)
    d = lambda p: jnp.dot(sel, p, preferred_element_type=F32)
    return d(hi) + d(mid) + d(lo)


def _seg_sum(x, ones_bd):
    hi = x.astype(BF16)
    lo = (x - hi.astype(F32)).astype(BF16)
    w = ones_bd.shape[0]
    d = lambda p: jnp.concatenate(
        [jnp.dot(p[:, i:i + w], ones_bd, preferred_element_type=F32) for i in range(0, x.shape[1], w)],
        axis=1)
    return d(hi) + d(lo)


def _div_pow2(x, d):
    assert d & (d - 1) == 0
    return jnp.right_shift(x, d.bit_length() - 1)


def _rms(x, g):
    return x * lax.rsqrt(jnp.mean(x * x, axis=-1, keepdims=True) + RMS_EPS) * g


def _sigmoid(x):
    return 1.0 / (1.0 + jnp.exp(-x))


def _softplus(x):
    return jnp.maximum(x, 0.0) + jnp.log(1.0 + jnp.exp(-jnp.abs(x)))


def _prep_kernel(x_ref, shift_ref, pool_ref, *rest, nb, tt, pos0, n_prev):
    rest = list(rest)
    pool_prev_ref = rest.pop(0) if n_prev else None
    (n1_ref, win_ref, mu_ref, dec0_ref, wdec_ref, a0_ref, wa_ref, wg_ref, kk_ref, ka_ref, ones_ref,
     wpool_ref, pscale_ref, wbup_ref,
     r_ref, lw_ref, k_ref, v_ref, a_ref, b_ref, g_ref, ga_ref, gbb_ref,
     shift_out_ref, pool_out_ref, cz_ref, cu_ref) = rest
    if n_prev:
        pool_out_ref[0:n_prev] = pool_prev_ref[...]
    tm = tt * nb
    cz = cz_ref.shape[0]
    cu = cu_ref.shape[0]
    t = pl.program_id(1)
    hist = cu // nb if nb == 1 else pool_ref.shape[0]

    def put(ref, val):
        if nb == 1:
            ref[...] = val.reshape(ref.shape)
        else:
            for i in range(tt):
                ref[:, i, :] = val[i * nb:(i + 1) * nb]

    @pl.when(t == 0)
    def _():
        cz_ref[...] = shift_ref[...].reshape(cz_ref.shape)
        past = pool_ref[...].reshape(hist * nb, D_POOL)
        if hist * nb < cu:
            past = jnp.concatenate([jnp.zeros((cu - hist * nb, D_POOL), F32), past], axis=0)
        cu_ref[...] = past

    if nb == 1:
        x = x_ref[...].reshape(tm, D_MODEL)
    else:
        x = jnp.concatenate([x_ref[:, i, :] for i in range(tt)], axis=0)
    hb = _rms(x, n1_ref[...]).astype(BF16)

    c0 = D_SHIFT + D_POOL
    z_rw = jnp.dot(hb, win_ref[:, 0:D_SHIFT], preferred_element_type=F32)
    u = jnp.dot(hb, win_ref[:, D_SHIFT:c0], preferred_element_type=F32)
    z_ga = jnp.dot(hb, win_ref[:, c0:c0 + D_MODEL], preferred_element_type=F32)
    z_gb = jnp.dot(hb, win_ref[:, c0 + D_MODEL:c0 + 2 * D_MODEL], preferred_element_type=F32)

    ext = jnp.concatenate([cz_ref[...], z_rw], axis=0)
    z_prev = pltpu.roll(ext, nb, 0)[cz:]
    zs = z_rw + (z_prev - z_rw) * mu_ref[...]
    new_cz = z_rw[tm - cz:]
    cz_ref[...] = new_cz
    shift_out_ref[...] = new_cz.reshape(shift_out_ref.shape)

    r = zs[:, 0:D_A]
    k = zs[:, D_A:2 * D_A]
    v = zs[:, 2 * D_A:3 * D_A]
    lwa = zs[:, 3 * D_A:3 * D_A + LORA_WA]
    lg = zs[:, 3 * D_A + LORA_WA:D_SHIFT]
    w_log = -_softplus(-(dec0_ref[...] + _dot(jnp.tanh(lwa), wdec_ref[...]))) - 0.5
    a_in = _sigmoid(a0_ref[...] + _dot(lwa, wa_ref[...]))
    kk = k * kk_ref[...]
    kk = kk / jnp.maximum(jnp.sqrt(_seg_sum(kk * kk, ones_ref[...])), L2_EPS)
    put(r_ref, r)
    put(lw_ref, -jnp.exp(w_log))
    put(k_ref, k * (1.0 + (a_in - 1.0) * ka_ref[...]))
    put(v_ref, v)
    put(a_ref, -kk)
    put(b_ref, kk * a_in)
    put(g_ref, _dot(_sigmoid(lg), wg_ref[...]))

    extu = jnp.concatenate([cu_ref[...], u], axis=0)
    new_cu = extu[tm:]
    cu_ref[...] = new_cu
    pool_out_ref[n_prev] = new_cu[cu - hist * nb:].reshape(pool_out_ref.shape[1:])
    row = lax.broadcasted_iota(jnp.int32, (tm, 1), 0)
    step = _div_pow2(row, nb)
    pos = pos0 + t * tt + step
    parts = []
    for gi, win in enumerate(POOL_WINDOWS):
        sl = slice(gi * POOL_GD, (gi + 1) * POOL_GD)
        s = extu[:, sl]
        span = 1
        while span < win:
            s = s + pltpu.roll(s, span * nb, 0)
            span *= 2
        cnt = jnp.minimum(pos + 1, win).astype(F32)
        p = s[cu:] / cnt - u[:, sl]
        parts.append(_dot(p, wpool_ref[gi]))
    pooled = jnp.concatenate(parts, axis=-1) * pscale_ref[...]
    b_out = _dot(pooled, wbup_ref[...])

    put(ga_ref, _sigmoid(z_ga))
    put(gbb_ref, _sigmoid(z_gb) * b_out)


def _layer_spec(w, l):
    li = min(l, w.shape[0] - 1)
    nd = w.ndim - 1
    return pl.BlockSpec((None,) + w.shape[1:], lambda *_: (li,) + (0,) * nd, pipeline_mode=pl.Buffered(1))


def _prep_call(x, shift_in, pool_in, pool_prev, wts, l, *, nb, tt, pos0, grid, row_block, row_map,
               shift_shape, shift_in_map, shift_out_map, pool_shape, pool_block, pool_in_spec, pool_stack_map):
    lead = x.shape[:-1]
    n_prev = 0 if pool_prev is None else pool_prev.shape[0]
    outs_rows = lambda w: jax.ShapeDtypeStruct(lead + (w,), F32)
    rb = lambda w: pl.BlockSpec(row_block + (w,), row_map)
    cz = max(SUBLANES, nb)
    cu = POOL_CARRY * nb
    shift_block = (1, cz, D_SHIFT)
    stacked = lambda n: pl.BlockSpec((n,) + pool_block, pool_stack_map)
    in_specs = ([rb(D_MODEL), pl.BlockSpec(shift_block, shift_in_map), pool_in_spec]
                + ([stacked(n_prev)] if n_prev else []) + [_layer_spec(w, l) for w in wts])
    out_shape = [outs_rows(D_A)] * 7 + [outs_rows(D_MODEL)] * 2 + [
        jax.ShapeDtypeStruct(shift_shape, F32), jax.ShapeDtypeStruct((n_prev + 1,) + pool_shape, F32)]
    out_specs = [rb(D_A)] * 7 + [rb(D_MODEL)] * 2 + [pl.BlockSpec(shift_block, shift_out_map),
                                                      stacked(n_prev + 1)]
    return pl.pallas_call(
        functools.partial(_prep_kernel, nb=nb, tt=tt, pos0=pos0, n_prev=n_prev),
        grid=grid, in_specs=in_specs, out_specs=out_specs, out_shape=out_shape,
        scratch_shapes=[pltpu.VMEM((cz, D_SHIFT), F32), pltpu.VMEM((cu, D_POOL), F32)],
        compiler_params=pltpu.CompilerParams(dimension_semantics=("arbitrary", "arbitrary"),
                                             vmem_limit_bytes=VMEM_LIMIT),
        name="prep",
    )(x, shift_in, pool_in, *([pool_prev] if n_prev else []), *wts)


def _scan_masks(steps):
    assert CHUNK == HEAD_DIM
    ri = lax.broadcasted_iota(jnp.int32, (CHUNK, LANES), 0)
    ci = jnp.bitwise_and(lax.broadcasted_iota(jnp.int32, (CHUNK, LANES), 1), CHUNK - 1)
    blk = lambda s: _div_pow2(ri, s) == _div_pow2(ci, s)
    same_seq = blk(steps)
    strict = same_seq & (ri > ci)
    incl = same_seq & (ri >= ci)
    levels = []
    size = SUBLANES
    while size < steps:
        size *= 2
        levels.append(strict & blk(size) & ~blk(size // 2))
    base = strict & blk(SUBLANES)
    eye = ri == ci
    return same_seq, strict, incl, base, levels, eye


def _bdot(a, b):
    return lax.dot_general(a.astype(BF16), b.astype(BF16), (((2,), (1,)), ((0,), (0,))),
                           preferred_element_type=F32)


def _bdot_nt(a, b):
    return lax.dot_general(a.astype(BF16), b.astype(BF16), (((2,), (2,)), ((0,), (0,))),
                           preferred_element_type=F32)


def _bdot_tn(a, b):
    return lax.dot_general(a.astype(BF16), b.astype(BF16), (((1,), (1,)), ((0,), (0,))),
                           preferred_element_type=F32)


def _head_stack(x):
    first = lax.broadcasted_iota(jnp.int32, (1, 1, LANES), 2) < HEAD_DIM
    return jnp.concatenate([jnp.where(first, x, 0.0), jnp.where(first, 0.0, x)], axis=1)


def _hdot(a, b):
    return _bdot(a, _head_stack(b))


def _tri_inverse(a_strict, base, levels, eye):
    ad = jnp.where(base, a_strict, 0.0)
    p2 = _hdot(ad, ad)
    t = jnp.where(eye, 1.0, ad)
    t = t + _hdot(t, p2)
    t = t + _hdot(t, _hdot(p2, p2))
    for lvl in levels:
        t = t + _hdot(t, _hdot(jnp.where(lvl, a_strict, 0.0), t))
    return t


def _scan_kernel(r_ref, lw_ref, k_ref, v_ref, a_ref, b_ref, h0_ref, *rest, steps, n_chunks, n_prev):
    y_ref, hout_ref, h_ref = rest[-3:]
    if n_prev:
        hout_ref[0:n_prev] = rest[0][...]
    nseq = CHUNK // steps
    nst = h_ref.shape[0]
    t = pl.program_id(1)

    @pl.when(t == 0)
    def _():
        zero = jnp.zeros((HEAD_DIM, HEAD_DIM), F32)
        for s in range(nst):
            for p in range(PAIRS):
                top = jnp.concatenate([h0_ref[0, s, p, 0], zero], axis=1)
                bot = jnp.concatenate([zero, h0_ref[0, s, p, 1]], axis=1)
                h_ref[s, p] = jnp.concatenate([top, bot], axis=0)

    same_seq, strict, incl, base, levels, eye = _scan_masks(steps)
    sel = jnp.concatenate([incl[:, :CHUNK], same_seq[:, :CHUNK]], axis=0).astype(BF16)
    hi = lax.broadcasted_iota(jnp.int32, (LANES, LANES), 0)
    hj = lax.broadcasted_iota(jnp.int32, (LANES, LANES), 1)
    same_head = _div_pow2(hi, HEAD_DIM) == _div_pow2(hj, HEAD_DIM)
    rows = n_chunks * CHUNK

    ld = lambda ref: ref[...].reshape(rows, D_A)
    r, lw, k, v, a, b = (ld(x) for x in (r_ref, lw_ref, k_ref, v_ref, a_ref, b_ref))
    cc = [_dot_sel(sel, lw[c * CHUNK:(c + 1) * CHUNK]) for c in range(n_chunks)]
    cs = jnp.concatenate([x[:CHUNK] for x in cc], axis=0)
    cl = jnp.concatenate([x[CHUNK:] for x in cc], axis=0)
    e_cs = jnp.exp(cs)
    e_ncs = jnp.exp(-cs)
    e_cl = jnp.exp(cl - cs)
    gam = jnp.exp(cl)

    def inst(x):
        x3 = x.reshape(n_chunks, CHUNK, D_A)
        return jnp.concatenate([x3[:, :, p * LANES:(p + 1) * LANES] for p in range(PAIRS)], axis=0)

    at = inst(a * jnp.exp(cs - lw))
    rt = inst(r * e_cs)
    bh = inst(b * e_cl)
    kh = inst(k * e_cl)
    vv = inst(v)
    aa = _bdot_nt(jnp.concatenate([at, rt], axis=1),
                  jnp.concatenate([_head_stack(inst(b * e_ncs)), _head_stack(inst(k * e_ncs))], axis=1))
    a_ab = jnp.where(strict, aa[:, :CHUNK, :LANES], 0.0)
    a_ak = jnp.where(strict, aa[:, :CHUNK, LANES:], 0.0)
    a_rb = jnp.where(incl, aa[:, CHUNK:, :LANES], 0.0)
    a_rk = jnp.where(incl, aa[:, CHUNK:, LANES:], 0.0)
    tinv = _tri_inverse(a_ab, base, levels, eye)
    akv = _hdot(jnp.concatenate([a_ak, a_rk], axis=1), vv)
    tx = _bdot(tinv, jnp.concatenate([_head_stack(at), _head_stack(akv[:, :CHUNK])], axis=2))
    ah = tx[:, :, :LANES]
    w1 = tx[:, :, LANES:]
    rx = _bdot(a_rb, jnp.concatenate([_head_stack(ah), _head_stack(w1)], axis=2))
    rh = rt + rx[:, :, :LANES]
    y1 = rx[:, :, LANES:] + akv[:, CHUNK:]

    ng = PAIRS * n_chunks
    bf16_rows = 2 * SUBLANES

    def seqsplit(x):
        parts = [x[:, s * steps:(s + 1) * steps] for s in range(nseq)]
        if steps < bf16_rows:
            pad = jnp.zeros((x.shape[0], bf16_rows - steps, x.shape[2]), F32)
            parts = [jnp.concatenate([q, pad], axis=1) for q in parts]
        return jnp.concatenate(parts, axis=0)

    bhs = seqsplit(bh)
    mp = jnp.where(same_head, _bdot_tn(seqsplit(ah), bhs), 0.0)
    np_ = jnp.where(same_head, _bdot_tn(jnp.concatenate([seqsplit(w1), seqsplit(vv)], axis=1),
                                        jnp.concatenate([bhs, seqsplit(kh)], axis=1)), 0.0)
    gam_i = inst(gam)
    gam_m = jnp.concatenate([gam_i[:, s * steps:s * steps + 1] for s in range(nseq)], axis=0)
    order = [(s, p, c) for s in range(nseq) for p in range(PAIRS) for c in range(n_chunks)]
    n_groups = nst // nseq
    cpg = n_chunks // n_groups
    lanes_of = [(q, s, p) for q in range(n_groups) for s in range(nseq) for p in range(PAIRS)]
    take = lambda x, pos: jnp.concatenate(
        [x[s * ng + p * n_chunks + q * cpg + pos][None] for (q, s, p) in lanes_of], axis=0)
    cur = jnp.concatenate([h_ref[q * nseq + s, p][None] for (q, s, p) in lanes_of], axis=0)
    starts = []
    for pos in range(cpg):
        starts.append(cur)
        cur = cur * take(gam_m, pos) + _bdot(cur, take(mp, pos)) + take(np_, pos)
    slot = {key: i for i, key in enumerate(lanes_of)}
    s_start = jnp.concatenate(
        [starts[c % cpg][slot[(c // cpg, s, p)]][None] for (s, p, c) in order], axis=0)
    finals = {(q * nseq + s, p): cur[i] for i, (q, s, p) in enumerate(lanes_of)}
    y = (_bdot_nt(seqsplit(rh), s_start) + seqsplit(y1))[:, :steps]
    for i, (s, p, c) in enumerate(order):
        r0 = c * CHUNK + s * steps
        if len(y_ref.shape) == 3:
            per = y_ref.shape[1]
            y_ref[r0 // per, r0 % per:r0 % per + steps, p * LANES:(p + 1) * LANES] = y[i]
        else:
            y_ref[r0:r0 + steps, p * LANES:(p + 1) * LANES] = y[i]
    for (s, p), h in finals.items():
        h_ref[s, p] = h
        hout_ref[n_prev, 0, s, p, 0] = h[:HEAD_DIM, :HEAD_DIM]
        hout_ref[n_prev, 0, s, p, 1] = h[HEAD_DIM:, HEAD_DIM:]


def _scan_call(vecs, h0, prev, l, *, steps, n_chunks, grid, row_block, row_map):
    nst = h0.shape[2]
    li = min(l, h0.shape[0] - 1)
    n_prev = 0 if prev is None else prev.shape[0]
    st_block = (1, nst, PAIRS, 2, HEAD_DIM, HEAD_DIM)
    rb = pl.BlockSpec(row_block + (D_A,), row_map)
    hb = pl.BlockSpec((None,) + st_block, lambda b, t: (li, b, 0, 0, 0, 0, 0))
    stacked = lambda n: pl.BlockSpec((n,) + st_block, lambda b, t: (0, b, 0, 0, 0, 0, 0))
    return pl.pallas_call(
        functools.partial(_scan_kernel, steps=steps, n_chunks=n_chunks, n_prev=n_prev),
        grid=grid, in_specs=[rb] * 6 + [hb] + ([stacked(n_prev)] if n_prev else []),
        out_specs=[rb, stacked(n_prev + 1)],
        out_shape=[jax.ShapeDtypeStruct(vecs[0].shape, F32),
                   jax.ShapeDtypeStruct((n_prev + 1,) + h0.shape[1:], F32)],
        scratch_shapes=[pltpu.VMEM((nst, PAIRS, LANES, LANES), F32)],
        compiler_params=pltpu.CompilerParams(dimension_semantics=("arbitrary", "arbitrary"),
                                             vmem_limit_bytes=VMEM_LIMIT),
        name="scan",
    )(*vecs, h0, *([prev] if n_prev else []))


def _post_kernel(y_ref, r_ref, k_ref, v_ref, g_ref, ga_ref, gbb_ref, x_ref, ones_ref, rk_ref, lng_ref,
                 lnb_ref, waup_ref, wo_ref, n2_ref, w1_ref, w2_ref, fn_ref, o_ref, *, final):
    ones = ones_ref[...]
    y = y_ref[...]
    mu = _seg_sum(y, ones) * (1.0 / HEAD_DIM)
    d = y - mu
    var = _seg_sum(d * d, ones) * (1.0 / HEAD_DIM)
    yn = d * lax.rsqrt(var + GN_EPS) * lng_ref[...] + lnb_ref[...]
    bonus = _seg_sum(r_ref[...] * k_ref[...] * rk_ref[...], ones) * v_ref[...]
    a_out = _dot((yn + bonus) * g_ref[...], waup_ref[...])
    merged = ga_ref[...] * a_out + gbb_ref[...]
    x1 = x_ref[...] + _dot(merged, wo_ref[...])
    hb = _rms(x1, n2_ref[...]).astype(BF16)
    x2 = x1
    for c in range(0, D_FF, FF_CHUNK):
        f = jnp.maximum(jnp.dot(hb, w1_ref[:, c:c + FF_CHUNK], preferred_element_type=F32), 0.0)
        x2 = x2 + _dot(f * f, w2_ref[c:c + FF_CHUNK, :])
    o_ref[...] = _rms(x2, fn_ref[...]) if final else x2


def _post_call(rows_in, wts, l, *, final, tm):
    n = rows_in[0].shape[0]
    rb = lambda a: pl.BlockSpec((tm, a.shape[-1]), lambda i: (i, 0))
    return pl.pallas_call(
        functools.partial(_post_kernel, final=final),
        grid=(n // tm,),
        in_specs=[rb(a) for a in rows_in] + [_layer_spec(w, l) for w in wts],
        out_specs=pl.BlockSpec((tm, D_MODEL), lambda i: (i, 0)),
        out_shape=jax.ShapeDtypeStruct((n, D_MODEL), F32),
        compiler_params=pltpu.CompilerParams(dimension_semantics=("arbitrary",),
                                             vmem_limit_bytes=VMEM_LIMIT),
        name="post",
    )(*rows_in, *wts)


def _stacked_weights(w_in, mu_shift, decay0, w_decay2, a0, w_a2, w_g2, k_k, k_a, r_k, ln_x_g, ln_x_b,
                     w_a_up, w_pool, pool_scale, w_b_up, w_o, norm1_g, norm2_g, w_ff1, w_ff2, final_norm_g):
    depth = w_in.shape[0]
    row = lambda a: a.reshape(a.shape[0], 1, -1)
    zpad = jnp.zeros((depth, LORA_WA // 2, D_A), F32)
    seg = jnp.arange(MXU_DIM) // HEAD_DIM
    ones_bd = (seg[:, None] == seg[None, :]).astype(BF16)[None]
    prep = [row(norm1_g), w_in.astype(BF16), row(mu_shift), row(decay0),
            jnp.concatenate([w_decay2, zpad], axis=1).astype(BF16), row(a0),
            jnp.concatenate([zpad, w_a2], axis=1).astype(BF16), w_g2.astype(BF16),
            row(k_k), row(k_a), ones_bd, w_pool.astype(BF16), row(pool_scale), w_b_up.astype(BF16)]
    post = [ones_bd, row(r_k), row(ln_x_g), row(ln_x_b), w_a_up.astype(BF16), w_o.astype(BF16),
            row(norm2_g), w_ff1.astype(BF16), w_ff2.astype(BF16), final_norm_g.reshape(1, 1, -1)]
    return prep, post


def kernel(x_prompt, x_sample, state_shift, state_pool, state_wkv, norm1_g, w_in, mu_shift, decay0,
           w_decay2, a0, w_a2, w_g2, k_k, k_a, r_k, ln_x_g, ln_x_b, w_a_up, w_pool, pool_scale, w_b_up,
           w_o, norm2_g, w_ff1, w_ff2, final_norm_g):
    depth = w_in.shape[0]
    bp, tp, _ = x_prompt.shape
    bs, ts, _ = x_sample.shape
    tt_p = 512
    nb_s = 32
    ts_p = 256
    nb_scan = 4
    tm_post = 512
    nc_s = 2
    nseq = CHUNK // ts
    hist = state_pool.shape[2]
    prep_w, post_w = _stacked_weights(
        w_in, mu_shift, decay0, w_decay2, a0, w_a2, w_g2, k_k, k_a, r_k, ln_x_g, ln_x_b, w_a_up, w_pool,
        pool_scale, w_b_up, w_o, norm1_g, norm2_g, w_ff1, w_ff2, final_norm_g)

    p_shift0 = jnp.zeros((bp, SUBLANES, D_SHIFT), F32)
    p_pool0 = jnp.zeros((bp, POOL_CARRY, D_POOL), F32)
    p_wkv0 = jnp.zeros((1, bp // nb_scan, nb_scan, PAIRS, 2, HEAD_DIM, HEAD_DIM), F32)
    s_pool0 = jnp.swapaxes(state_pool, 1, 2)
    s_wkv0 = state_wkv.reshape(depth, bs // (nseq * nc_s), nseq * nc_s, PAIRS, 2, HEAD_DIM, HEAD_DIM)

    xp = x_prompt
    xs = x_sample.reshape(bs * ts, D_MODEL)
    outs = {k: [] for k in ("p_shift", "s_shift")}
    p_pool = p_wkv = s_pool = s_wkv = None
    for l in range(depth):
        final = l == depth - 1

        res = _prep_call(
            xp, p_shift0, p_pool0, p_pool, prep_w, l, nb=1, tt=tt_p, pos0=0, grid=(bp, tp // tt_p),
            row_block=(1, tt_p), row_map=lambda b, t: (b, t, 0),
            shift_shape=p_shift0.shape, shift_in_map=lambda b, t: (b, 0, 0),
            shift_out_map=lambda b, t: (b, 0, 0), pool_shape=p_pool0.shape,
            pool_block=(1, POOL_CARRY, D_POOL),
            pool_in_spec=pl.BlockSpec((1, POOL_CARRY, D_POOL), lambda b, t: (b, 0, 0)),
            pool_stack_map=lambda b, t: (0, b, 0, 0))
        r, lw, k, v, a, b, g, ga, gbb, sh, p_pool = res
        outs["p_shift"].append(sh[:, SUBLANES - 1])
        y, p_wkv = _scan_call(
            (r, lw, k, v, a, b), p_wkv0, p_wkv, l, steps=CHUNK, n_chunks=nb_scan * ts_p // CHUNK,
            grid=(bp // nb_scan, tp // ts_p), row_block=(nb_scan, ts_p), row_map=lambda b, t: (b, t, 0))
        flat = lambda z: z.reshape(bp * tp, z.shape[-1])
        xp = _post_call([flat(z) for z in (y, r, k, v, g, ga, gbb, xp)], post_w, l, final=final,
                        tm=tm_post).reshape(bp, tp, D_MODEL)

        res = _prep_call(
            xs.reshape(bs, ts, D_MODEL), state_shift, s_pool0, s_pool, prep_w, l, nb=nb_s, tt=ts,
            pos0=PAST_LEN, grid=(bs // nb_s, 1), row_block=(nb_s, ts), row_map=lambda b, t: (b, 0, 0),
            shift_shape=(1, bs, D_SHIFT), shift_in_map=lambda b, t, l=l: (l, b, 0),
            shift_out_map=lambda b, t: (0, b, 0), pool_shape=(hist, bs, D_POOL),
            pool_block=(hist, nb_s, D_POOL),
            pool_in_spec=pl.BlockSpec((None, hist, nb_s, D_POOL), lambda b, t, l=l: (l, 0, b, 0)),
            pool_stack_map=lambda b, t: (0, 0, b, 0))
        r, lw, k, v, a, b, g, ga, gbb = (z.reshape(bs * ts, z.shape[-1]) for z in res[:9])
        outs["s_shift"].append(res[9][0])
        s_pool = res[10]
        y, s_wkv = _scan_call(
            (r, lw, k, v, a, b), s_wkv0, s_wkv, l, steps=ts, n_chunks=nc_s,
            grid=(bs // (nseq * nc_s), 1), row_block=(nc_s * CHUNK,), row_map=lambda b, t: (b, 0))
        xs = _post_call([y, r, k, v, g, ga, gbb, xs], post_w, l, final=final, tm=tm_post)

    st = lambda key: jnp.stack(outs[key])
    wkv = lambda h, n: h.reshape(depth, n, N_HEADS, HEAD_DIM, HEAD_DIM)
    return (xp, xs.reshape(bs, ts, D_MODEL), st("p_shift"), p_pool[:, :, 1:], wkv(p_wkv, bp),
            st("s_shift"), jnp.swapaxes(s_pool, 1, 2), wkv(s_wkv, bs))
```

```python
import functools

import jax
import jax.numpy as jnp
from jax import lax
from jax.experimental import pallas as pl
from jax.experimental.pallas import tpu as pltpu

F32 = jnp.float32
BF16 = jnp.bfloat16

D_MODEL = 1024
HEAD_DIM = 64
D_A = 512
N_HEADS = 8
LORA_WA = 128
LORA_G = 128
D_SHIFT = 3 * D_A + LORA_WA + LORA_G
D_POOL = 512
POOL_WINDOWS = (2, 4, 8, 16)
POOL_GD = 128
POOL_CARRY = 16
D_FF = 4096
PAST_LEN = 16384
RMS_EPS = 1e-6
GN_EPS = HEAD_DIM * 1e-5
L2_EPS = 1e-12

LANES = 128
SUBLANES = 8
MXU_DIM = 256
CHUNK = 64
FF_CHUNK = 1024
PAIRS = D_A // LANES
SC_R, SC_LW, SC_K, SC_V, SC_A, SC_B, SC_WIDTH = (i * D_A for i in range(7))
PG_G, PG_GA, PG_GBB, PG_WIDTH = 0, D_A, D_A + D_MODEL, D_A + 2 * D_MODEL
VMEM_LIMIT = 56 * 1024 * 1024


def _dot(a, b):
    return jnp.dot(a.astype(BF16), b.astype(BF16), preferred_element_type=F32)


def _seg_sum(x, ones_bd):
    hi = x.astype(BF16)
    lo = (x - hi.astype(F32)).astype(BF16)
    w = ones_bd.shape[0]
    d = lambda p: jnp.concatenate(
        [jnp.dot(p[:, i:i + w], ones_bd, preferred_element_type=F32) for i in range(0, x.shape[1], w)],
        axis=1)
    return d(hi) + d(lo)


def _div_pow2(x, d):
    assert d & (d - 1) == 0
    return jnp.right_shift(x, d.bit_length() - 1)


def _rms(x, g):
    return x * lax.rsqrt(jnp.mean(x * x, axis=-1, keepdims=True) + RMS_EPS) * g


def _sigmoid(x):
    return 1.0 / (1.0 + jnp.exp(-x))


def _softplus(x):
    return jnp.maximum(x, 0.0) + jnp.log(1.0 + jnp.exp(-jnp.abs(x)))


def _prep_kernel(x_ref, shift_ref, pool_ref, *rest, nb, tt, pos0, n_prev):
    rest = list(rest)
    pool_prev_ref = rest.pop(0) if n_prev else None
    (n1_ref, win_ref, mu_ref, dec0_ref, wdec_ref, a0_ref, wa_ref, wg_ref, kk_ref, ka_ref, ones_ref,
     wpool_ref, pscale_ref, wbup_ref,
     sc_ref, pg_ref,
     shift_out_ref, pool_out_ref, cz_ref, cu_ref) = rest
    if n_prev:
        pool_out_ref[0:n_prev] = pool_prev_ref[...]
    tm = tt * nb
    cz = cz_ref.shape[0]
    cu = cu_ref.shape[0]
    t = pl.program_id(1)
    hist = cu // nb if nb == 1 else pool_ref.shape[0]

    def put(ref, col, val):
        cols = slice(col, col + val.shape[1])
        if nb == 1:
            ref[0, :, cols] = val
        else:
            for i in range(tt):
                ref[:, i, cols] = val[i * nb:(i + 1) * nb]

    @pl.when(t == 0)
    def _():
        cz_ref[...] = shift_ref[...].reshape(cz_ref.shape)
        past = pool_ref[...].reshape(hist * nb, D_POOL)
        if hist * nb < cu:
            past = jnp.concatenate([jnp.zeros((cu - hist * nb, D_POOL), F32), past], axis=0)
        cu_ref[...] = past

    if nb == 1:
        x = x_ref[...].reshape(tm, D_MODEL)
    else:
        x = jnp.concatenate([x_ref[:, i, :] for i in range(tt)], axis=0)
    hb = _rms(x, n1_ref[...]).astype(BF16)

    c0 = D_SHIFT + D_POOL
    z_rw = jnp.dot(hb, win_ref[:, 0:D_SHIFT], preferred_element_type=F32)
    u = jnp.dot(hb, win_ref[:, D_SHIFT:c0], preferred_element_type=F32)
    z_ga = jnp.dot(hb, win_ref[:, c0:c0 + D_MODEL], preferred_element_type=F32)
    z_gb = jnp.dot(hb, win_ref[:, c0 + D_MODEL:c0 + 2 * D_MODEL], preferred_element_type=F32)

    ext = jnp.concatenate([cz_ref[...], z_rw], axis=0)
    z_prev = pltpu.roll(ext, nb, 0)[cz:]
    zs = z_rw + (z_prev - z_rw) * mu_ref[...]
    new_cz = z_rw[tm - cz:]
    cz_ref[...] = new_cz
    shift_out_ref[...] = new_cz.reshape(shift_out_ref.shape)

    r = zs[:, 0:D_A]
    k = zs[:, D_A:2 * D_A]
    v = zs[:, 2 * D_A:3 * D_A]
    lwa = zs[:, 3 * D_A:3 * D_A + LORA_WA]
    lg = zs[:, 3 * D_A + LORA_WA:D_SHIFT]
    w_log = -_softplus(-(dec0_ref[...] + _dot(jnp.tanh(lwa), wdec_ref[...]))) - 0.5
    a_in = _sigmoid(a0_ref[...] + _dot(lwa, wa_ref[...]))
    kk = k * kk_ref[...]
    kk = kk / jnp.maximum(jnp.sqrt(_seg_sum(kk * kk, ones_ref[...])), L2_EPS)
    put(sc_ref, SC_R, r)
    put(sc_ref, SC_LW, -jnp.exp(w_log))
    put(sc_ref, SC_K, k * (1.0 + (a_in - 1.0) * ka_ref[...]))
    put(sc_ref, SC_V, v)
    put(sc_ref, SC_A, -kk)
    put(sc_ref, SC_B, kk * a_in)
    put(pg_ref, PG_G, _dot(_sigmoid(lg), wg_ref[...]))

    extu = jnp.concatenate([cu_ref[...], u], axis=0)
    new_cu = extu[tm:]
    cu_ref[...] = new_cu
    pool_out_ref[n_prev] = new_cu[cu - hist * nb:].reshape(pool_out_ref.shape[1:])
    row = lax.broadcasted_iota(jnp.int32, (tm, 1), 0)
    step = _div_pow2(row, nb)
    pos = pos0 + t * tt + step
    parts = []
    for gi, win in enumerate(POOL_WINDOWS):
        sl = slice(gi * POOL_GD, (gi + 1) * POOL_GD)
        s = extu[:, sl]
        span = 1
        while span < win:
            s = s + pltpu.roll(s, span * nb, 0)
            span *= 2
        cnt = jnp.minimum(pos + 1, win).astype(F32)
        p = s[cu:] / cnt - u[:, sl]
        parts.append(_dot(p, wpool_ref[gi]))
    pooled = jnp.concatenate(parts, axis=-1) * pscale_ref[...]
    b_out = _dot(pooled, wbup_ref[...])

    put(pg_ref, PG_GA, _sigmoid(z_ga))
    put(pg_ref, PG_GBB, _sigmoid(z_gb) * b_out)


def _layer_spec(w, l):
    li = min(l, w.shape[0] - 1)
    nd = w.ndim - 1
    return pl.BlockSpec((None,) + w.shape[1:], lambda *_: (li,) + (0,) * nd, pipeline_mode=pl.Buffered(1))


def _prep_call(x, shift_in, pool_in, pool_prev, wts, l, *, nb, tt, pos0, grid, row_block, row_map,
               shift_shape, shift_in_map, shift_out_map, pool_shape, pool_block, pool_in_spec, pool_stack_map):
    lead = x.shape[:-1]
    n_prev = 0 if pool_prev is None else pool_prev.shape[0]
    outs_rows = lambda w: jax.ShapeDtypeStruct(lead + (w,), F32)
    rb = lambda w: pl.BlockSpec(row_block + (w,), row_map)
    cz = max(SUBLANES, nb)
    cu = POOL_CARRY * nb
    shift_block = (1, cz, D_SHIFT)
    stacked = lambda n: pl.BlockSpec((n,) + pool_block, pool_stack_map)
    in_specs = ([rb(D_MODEL), pl.BlockSpec(shift_block, shift_in_map), pool_in_spec]
                + ([stacked(n_prev)] if n_prev else []) + [_layer_spec(w, l) for w in wts])
    out_shape = [outs_rows(SC_WIDTH), outs_rows(PG_WIDTH),
                 jax.ShapeDtypeStruct(shift_shape, F32), jax.ShapeDtypeStruct((n_prev + 1,) + pool_shape, F32)]
    out_specs = [rb(SC_WIDTH), rb(PG_WIDTH), pl.BlockSpec(shift_block, shift_out_map), stacked(n_prev + 1)]
    return pl.pallas_call(
        functools.partial(_prep_kernel, nb=nb, tt=tt, pos0=pos0, n_prev=n_prev),
        grid=grid, in_specs=in_specs, out_specs=out_specs, out_shape=out_shape,
        scratch_shapes=[pltpu.VMEM((cz, D_SHIFT), F32), pltpu.VMEM((cu, D_POOL), F32)],
        compiler_params=pltpu.CompilerParams(dimension_semantics=("arbitrary", "arbitrary"),
                                             vmem_limit_bytes=VMEM_LIMIT),
        name="prep",
    )(x, shift_in, pool_in, *([pool_prev] if n_prev else []), *wts)


def _scan_masks(steps):
    assert CHUNK == HEAD_DIM
    ri = lax.broadcasted_iota(jnp.int32, (CHUNK, LANES), 0)
    ci = jnp.bitwise_and(lax.broadcasted_iota(jnp.int32, (CHUNK, LANES), 1), CHUNK - 1)
    blk = lambda s: _div_pow2(ri, s) == _div_pow2(ci, s)
    same_seq = blk(steps)
    strict = same_seq & (ri > ci)
    incl = same_seq & (ri >= ci)
    levels = []
    size = SUBLANES
    while size < steps:
        size *= 2
        levels.append(strict & blk(size) & ~blk(size // 2))
    base = strict & blk(SUBLANES)
    eye = ri == ci
    return same_seq, strict, incl, base, levels, eye


def _bdot(a, b):
    return lax.dot_general(a.astype(BF16), b.astype(BF16), (((2,), (1,)), ((0,), (0,))),
                           preferred_element_type=F32)


def _bdot_nt(a, b):
    return lax.dot_general(a.astype(BF16), b.astype(BF16), (((2,), (2,)), ((0,), (0,))),
                           preferred_element_type=F32)


def _bdot_tn(a, b):
    return lax.dot_general(a.astype(BF16), b.astype(BF16), (((1,), (1,)), ((0,), (0,))),
                           preferred_element_type=F32)


def _head_stack(x):
    first = lax.broadcasted_iota(jnp.int32, (1, 1, LANES), 2) < HEAD_DIM
    return jnp.concatenate([jnp.where(first, x, 0.0), jnp.where(first, 0.0, x)], axis=1)


def _hdot(a, b):
    return _bdot(a, _head_stack(b))


def _tri_inverse(a_strict, base, levels, eye):
    ad = jnp.where(base, a_strict, 0.0)
    p2 = _hdot(ad, ad)
    t = jnp.where(eye, 1.0, ad)
    t = t + _hdot(t, p2)
    t = t + _hdot(t, _hdot(p2, p2))
    for lvl in levels:
        t = t + _hdot(t, _hdot(jnp.where(lvl, a_strict, 0.0), t))
    return t


def _scan_kernel(sc_ref, h0_ref, *rest, steps, n_chunks, n_prev):
    y_ref, hout_ref, h_ref = rest[-3:]
    if n_prev:
        hout_ref[0:n_prev] = rest[0][...]
    nseq = CHUNK // steps
    nst = h_ref.shape[0]
    t = pl.program_id(1)

    @pl.when(t == 0)
    def _():
        zero = jnp.zeros((HEAD_DIM, HEAD_DIM), F32)
        for s in range(nst):
            for p in range(PAIRS):
                top = jnp.concatenate([h0_ref[0, s, p, 0], zero], axis=1)
                bot = jnp.concatenate([zero, h0_ref[0, s, p, 1]], axis=1)
                h_ref[s, p] = jnp.concatenate([top, bot], axis=0)

    _, strict, incl, base, levels, eye = _scan_masks(steps)
    hi = lax.broadcasted_iota(jnp.int32, (LANES, LANES), 0)
    hj = lax.broadcasted_iota(jnp.int32, (LANES, LANES), 1)
    same_head = _div_pow2(hi, HEAD_DIM) == _div_pow2(hj, HEAD_DIM)
    rows = n_chunks * CHUNK

    sc = sc_ref[...].reshape(rows, SC_WIDTH)
    r, lw, k, v, a, b = (sc[:, c0:c0 + D_A] for c0 in (SC_R, SC_LW, SC_K, SC_V, SC_A, SC_B))
    pos = jnp.bitwise_and(lax.broadcasted_iota(jnp.int32, (rows, 1), 0), steps - 1)
    cs = lw
    span = 1
    while span < steps:
        cs = cs + jnp.where(pos >= span, pltpu.roll(cs, span, 0), 0.0)
        span *= 2
    cl = jnp.broadcast_to(cs.reshape(rows // steps, steps, D_A)[:, steps - 1:steps, :],
                          (rows // steps, steps, D_A)).reshape(rows, D_A)
    e_cs = jnp.exp(cs)
    e_ncs = jnp.exp(-cs)
    e_cl = jnp.exp(cl - cs)
    gam = jnp.exp(cl)

    def inst(x):
        x3 = x.reshape(n_chunks, CHUNK, D_A)
        return jnp.concatenate([x3[:, :, p * LANES:(p + 1) * LANES] for p in range(PAIRS)], axis=0)

    at = inst(a * jnp.exp(cs - lw))
    rt = inst(r * e_cs)
    bh = inst(b * e_cl)
    kh = inst(k * e_cl)
    vv = inst(v)
    aa = _bdot_nt(jnp.concatenate([at, rt], axis=1),
                  jnp.concatenate([_head_stack(inst(b * e_ncs)), _head_stack(inst(k * e_ncs))], axis=1))
    a_ab = jnp.where(strict, aa[:, :CHUNK, :LANES], 0.0)
    a_ak = jnp.where(strict, aa[:, :CHUNK, LANES:], 0.0)
    a_rb = jnp.where(incl, aa[:, CHUNK:, :LANES], 0.0)
    a_rk = jnp.where(incl, aa[:, CHUNK:, LANES:], 0.0)
    tinv = _tri_inverse(a_ab, base, levels, eye)
    akv = _hdot(jnp.concatenate([a_ak, a_rk], axis=1), vv)
    tx = _bdot(tinv, jnp.concatenate([_head_stack(at), _head_stack(akv[:, :CHUNK])], axis=2))
    ah = tx[:, :, :LANES]
    w1 = tx[:, :, LANES:]
    rx = _bdot(a_rb, jnp.concatenate([_head_stack(ah), _head_stack(w1)], axis=2))
    rh = rt + rx[:, :, :LANES]
    y1 = rx[:, :, LANES:] + akv[:, CHUNK:]

    ng = PAIRS * n_chunks
    bf16_rows = 2 * SUBLANES

    def seqsplit(x):
        parts = [x[:, s * steps:(s + 1) * steps] for s in range(nseq)]
        if steps < bf16_rows:
            pad = jnp.zeros((x.shape[0], bf16_rows - steps, x.shape[2]), F32)
            parts = [jnp.concatenate([q, pad], axis=1) for q in parts]
        return jnp.concatenate(parts, axis=0)

    bhs = seqsplit(bh)
    mp = jnp.where(same_head, _bdot_tn(seqsplit(ah), bhs), 0.0)
    np_ = jnp.where(same_head, _bdot_tn(jnp.concatenate([seqsplit(w1), seqsplit(vv)], axis=1),
                                        jnp.concatenate([bhs, seqsplit(kh)], axis=1)), 0.0)
    gam_i = inst(gam)
    gam_m = jnp.concatenate([gam_i[:, s * steps:s * steps + 1] for s in range(nseq)], axis=0)
    order = [(s, p, c) for s in range(nseq) for p in range(PAIRS) for c in range(n_chunks)]
    n_groups = nst // nseq
    cpg = n_chunks // n_groups
    lanes_of = [(q, s, p) for q in range(n_groups) for s in range(nseq) for p in range(PAIRS)]
    take = lambda x, pos: jnp.concatenate(
        [x[s * ng + p * n_chunks + q * cpg + pos][None] for (q, s, p) in lanes_of], axis=0)
    cur = jnp.concatenate([h_ref[q * nseq + s, p][None] for (q, s, p) in lanes_of], axis=0)
    starts = []
    for pos in range(cpg):
        starts.append(cur)
        cur = cur * take(gam_m, pos) + _bdot(cur, take(mp, pos)) + take(np_, pos)
    slot = {key: i for i, key in enumerate(lanes_of)}
    s_start = jnp.concatenate(
        [starts[c % cpg][slot[(c // cpg, s, p)]][None] for (s, p, c) in order], axis=0)
    finals = {(q * nseq + s, p): cur[i] for i, (q, s, p) in enumerate(lanes_of)}
    y = (_bdot_nt(seqsplit(rh), s_start) + seqsplit(y1))[:, :steps]
    for i, (s, p, c) in enumerate(order):
        r0 = c * CHUNK + s * steps
        if len(y_ref.shape) == 3:
            per = y_ref.shape[1]
            y_ref[r0 // per, r0 % per:r0 % per + steps, p * LANES:(p + 1) * LANES] = y[i]
        else:
            y_ref[r0:r0 + steps, p * LANES:(p + 1) * LANES] = y[i]
    for (s, p), h in finals.items():
        h_ref[s, p] = h
        hout_ref[n_prev, 0, s, p, 0] = h[:HEAD_DIM, :HEAD_DIM]
        hout_ref[n_prev, 0, s, p, 1] = h[HEAD_DIM:, HEAD_DIM:]


def _scan_call(sc, h0, prev, l, *, steps, n_chunks, grid, row_block, row_map):
    nst = h0.shape[2]
    li = min(l, h0.shape[0] - 1)
    n_prev = 0 if prev is None else prev.shape[0]
    st_block = (1, nst, PAIRS, 2, HEAD_DIM, HEAD_DIM)
    rb = lambda w: pl.BlockSpec(row_block + (w,), row_map)
    hb = pl.BlockSpec((None,) + st_block, lambda b, t: (li, b, 0, 0, 0, 0, 0))
    stacked = lambda n: pl.BlockSpec((n,) + st_block, lambda b, t: (0, b, 0, 0, 0, 0, 0))
    return pl.pallas_call(
        functools.partial(_scan_kernel, steps=steps, n_chunks=n_chunks, n_prev=n_prev),
        grid=grid, in_specs=[rb(SC_WIDTH), hb] + ([stacked(n_prev)] if n_prev else []),
        out_specs=[rb(D_A), stacked(n_prev + 1)],
        out_shape=[jax.ShapeDtypeStruct(sc.shape[:-1] + (D_A,), F32),
                   jax.ShapeDtypeStruct((n_prev + 1,) + h0.shape[1:], F32)],
        scratch_shapes=[pltpu.VMEM((nst, PAIRS, LANES, LANES), F32)],
        compiler_params=pltpu.CompilerParams(dimension_semantics=("arbitrary", "arbitrary"),
                                             vmem_limit_bytes=VMEM_LIMIT),
        name="scan",
    )(sc, h0, *([prev] if n_prev else []))


def _post_kernel(y_ref, r_ref, k_ref, v_ref, pg_ref, x_ref, ones_ref, rk_ref, lng_ref,
                 lnb_ref, waup_ref, wo_ref, n2_ref, w1_ref, w2_ref, fn_ref, o_ref, *, final):
    ones = ones_ref[...]
    y = y_ref[...]
    mu = _seg_sum(y, ones) * (1.0 / HEAD_DIM)
    d = y - mu
    var = _seg_sum(d * d, ones) * (1.0 / HEAD_DIM)
    yn = d * lax.rsqrt(var + GN_EPS) * lng_ref[...] + lnb_ref[...]
    bonus = _seg_sum(r_ref[...] * k_ref[...] * rk_ref[...], ones) * v_ref[...]
    a_out = _dot((yn + bonus) * pg_ref[:, PG_G:PG_GA], waup_ref[...])
    merged = pg_ref[:, PG_GA:PG_GBB] * a_out + pg_ref[:, PG_GBB:PG_WIDTH]
    x1 = x_ref[...] + _dot(merged, wo_ref[...])
    hb = _rms(x1, n2_ref[...]).astype(BF16)
    x2 = x1
    for c in range(0, D_FF, FF_CHUNK):
        f = jnp.maximum(jnp.dot(hb, w1_ref[:, c:c + FF_CHUNK], preferred_element_type=F32), 0.0)
        x2 = x2 + _dot(f * f, w2_ref[c:c + FF_CHUNK, :])
    o_ref[...] = _rms(x2, fn_ref[...]) if final else x2


def _post_call(y, sc, pg, x, wts, l, *, final, tm):
    n = y.shape[0]
    rb = lambda a: pl.BlockSpec((tm, a.shape[-1]), lambda i: (i, 0))
    section = lambda c0: pl.BlockSpec((tm, D_A), lambda i: (i, c0 // D_A))
    return pl.pallas_call(
        functools.partial(_post_kernel, final=final),
        grid=(n // tm,),
        in_specs=[rb(y), section(SC_R), section(SC_K), section(SC_V), rb(pg), rb(x)]
                 + [_layer_spec(w, l) for w in wts],
        out_specs=pl.BlockSpec((tm, D_MODEL), lambda i: (i, 0)),
        out_shape=jax.ShapeDtypeStruct((n, D_MODEL), F32),
        compiler_params=pltpu.CompilerParams(dimension_semantics=("arbitrary",),
                                             vmem_limit_bytes=VMEM_LIMIT),
        name="post",
    )(y, sc, sc, sc, pg, x, *wts)


def _stacked_weights(w_in, mu_shift, decay0, w_decay2, a0, w_a2, w_g2, k_k, k_a, r_k, ln_x_g, ln_x_b,
                     w_a_up, w_pool, pool_scale, w_b_up, w_o, norm1_g, norm2_g, w_ff1, w_ff2, final_norm_g):
    depth = w_in.shape[0]
    row = lambda a: a.reshape(a.shape[0], 1, -1)
    zpad = jnp.zeros((depth, LORA_WA // 2, D_A), F32)
    seg = jnp.arange(MXU_DIM) // HEAD_DIM
    ones_bd = (seg[:, None] == seg[None, :]).astype(BF16)[None]
    prep = [row(norm1_g), w_in.astype(BF16), row(mu_shift), row(decay0),
            jnp.concatenate([w_decay2, zpad], axis=1).astype(BF16), row(a0),
            jnp.concatenate([zpad, w_a2], axis=1).astype(BF16), w_g2.astype(BF16),
            row(k_k), row(k_a), ones_bd, w_pool.astype(BF16), row(pool_scale), w_b_up.astype(BF16)]
    post = [ones_bd, row(r_k), row(ln_x_g), row(ln_x_b), w_a_up.astype(BF16), w_o.astype(BF16),
            row(norm2_g), w_ff1.astype(BF16), w_ff2.astype(BF16), final_norm_g.reshape(1, 1, -1)]
    return prep, post


def kernel(x_prompt, x_sample, state_shift, state_pool, state_wkv, norm1_g, w_in, mu_shift, decay0,
           w_decay2, a0, w_a2, w_g2, k_k, k_a, r_k, ln_x_g, ln_x_b, w_a_up, w_pool, pool_scale, w_b_up,
           w_o, norm2_g, w_ff1, w_ff2, final_norm_g):
    depth = w_in.shape[0]
    bp, tp, _ = x_prompt.shape
    bs, ts, _ = x_sample.shape
    tt_p = 512
    nb_s = 32
    ts_p = 256
    nb_scan = 4
    tm_post = 512
    nc_s = 2
    nseq = CHUNK // ts
    hist = state_pool.shape[2]
    prep_w, post_w = _stacked_weights(
        w_in, mu_shift, decay0, w_decay2, a0, w_a2, w_g2, k_k, k_a, r_k, ln_x_g, ln_x_b, w_a_up, w_pool,
        pool_scale, w_b_up, w_o, norm1_g, norm2_g, w_ff1, w_ff2, final_norm_g)

    p_shift0 = jnp.zeros((bp, SUBLANES, D_SHIFT), F32)
    p_pool0 = jnp.zeros((bp, POOL_CARRY, D_POOL), F32)
    p_wkv0 = jnp.zeros((1, bp // nb_scan, nb_scan, PAIRS, 2, HEAD_DIM, HEAD_DIM), F32)
    s_pool0 = jnp.swapaxes(state_pool, 1, 2)
    s_wkv0 = state_wkv.reshape(depth, bs // (nseq * nc_s), nseq * nc_s, PAIRS, 2, HEAD_DIM, HEAD_DIM)

    xp = x_prompt
    xs = x_sample.reshape(bs * ts, D_MODEL)
    outs = {k: [] for k in ("p_shift", "s_shift")}
    p_pool = p_wkv = s_pool = s_wkv = None
    for l in range(depth):
        final = l == depth - 1

        res = _prep_call(
            xp, p_shift0, p_pool0, p_pool, prep_w, l, nb=1, tt=tt_p, pos0=0, grid=(bp, tp // tt_p),
            row_block=(1, tt_p), row_map=lambda b, t: (b, t, 0),
            shift_shape=p_shift0.shape, shift_in_map=lambda b, t: (b, 0, 0),
            shift_out_map=lambda b, t: (b, 0, 0), pool_shape=p_pool0.shape,
            pool_block=(1, POOL_CARRY, D_POOL),
            pool_in_spec=pl.BlockSpec((1, POOL_CARRY, D_POOL), lambda b, t: (b, 0, 0)),
            pool_stack_map=lambda b, t: (0, b, 0, 0))
        sc, pg, sh, p_pool = res
        outs["p_shift"].append(sh[:, SUBLANES - 1])
        y, p_wkv = _scan_call(
            sc, p_wkv0, p_wkv, l, steps=CHUNK, n_chunks=nb_scan * ts_p // CHUNK,
            grid=(bp // nb_scan, tp // ts_p), row_block=(nb_scan, ts_p), row_map=lambda b, t: (b, t, 0))
        flat = lambda z: z.reshape(bp * tp, z.shape[-1])
        xp = _post_call(flat(y), flat(sc), flat(pg), flat(xp), post_w, l, final=final,
                        tm=tm_post).reshape(bp, tp, D_MODEL)

        res = _prep_call(
            xs.reshape(bs, ts, D_MODEL), state_shift, s_pool0, s_pool, prep_w, l, nb=nb_s, tt=ts,
            pos0=PAST_LEN, grid=(bs // nb_s, 1), row_block=(nb_s, ts), row_map=lambda b, t: (b, 0, 0),
            shift_shape=(1, bs, D_SHIFT), shift_in_map=lambda b, t, l=l: (l, b, 0),
            shift_out_map=lambda b, t: (0, b, 0), pool_shape=(hist, bs, D_POOL),
            pool_block=(hist, nb_s, D_POOL),
            pool_in_spec=pl.BlockSpec((None, hist, nb_s, D_POOL), lambda b, t, l=l: (l, 0, b, 0)),
            pool_stack_map=lambda b, t: (0, 0, b, 0))
        sc, pg = (z.reshape(bs * ts, z.shape[-1]) for z in res[:2])
        outs["s_shift"].append(res[2][0])
        s_pool = res[3]
        y, s_wkv = _scan_call(
            sc, s_wkv0, s_wkv, l, steps=ts, n_chunks=nc_s,
            grid=(bs // (nseq * nc_s), 1), row_block=(nc_s * CHUNK,), row_map=lambda b, t: (b, 0))
        xs = _post_call(y, sc, pg, xs, post_w, l, final=final, tm=tm_post)

    st = lambda key: jnp.stack(outs[key])
    wkv = lambda h, n: h.reshape(depth, n, N_HEADS, HEAD_DIM, HEAD_DIM)
    return (xp, xs.reshape(bs, ts, D_MODEL), st("p_shift"), p_pool[:, :, 1:], wkv(p_wkv, bp),
            st("s_shift"), jnp.swapaxes(s_pool, 1, 2), wkv(s_wkv, bs))
```

```python
import functools

import jax
import jax.numpy as jnp
from jax import lax
from jax.experimental import pallas as pl
from jax.experimental.pallas import tpu as pltpu

F32 = jnp.float32
BF16 = jnp.bfloat16

D_MODEL = 1024
HEAD_DIM = 64
D_A = 512
N_HEADS = 8
LORA_WA = 128
LORA_G = 128
D_SHIFT = 3 * D_A + LORA_WA + LORA_G
D_POOL = 512
POOL_WINDOWS = (2, 4, 8, 16)
POOL_GD = 128
POOL_CARRY = 16
D_FF = 4096
PAST_LEN = 16384
RMS_EPS = 1e-6
GN_EPS = HEAD_DIM * 1e-5
L2_EPS = 1e-12

LANES = 128
SUBLANES = 8
MXU_DIM = 256
CHUNK = 64
FF_CHUNK = 1024
PAIRS = D_A // LANES
SC_R, SC_LW, SC_K, SC_V, SC_A, SC_B, SC_WIDTH = (i * D_A for i in range(7))
PG_G, PG_GA, PG_GBB, PG_WIDTH = 0, D_A, D_A + D_MODEL, D_A + 2 * D_MODEL
VMEM_LIMIT = 56 * 1024 * 1024


def _dot(a, b):
    return jnp.dot(a.astype(BF16), b.astype(BF16), preferred_element_type=F32)


def _seg_sum(x, ones_bd):
    hi = x.astype(BF16)
    lo = (x - hi.astype(F32)).astype(BF16)
    w = ones_bd.shape[0]
    d = lambda p: jnp.concatenate(
        [jnp.dot(p[:, i:i + w], ones_bd, preferred_element_type=F32) for i in range(0, x.shape[1], w)],
        axis=1)
    return d(hi) + d(lo)


def _div_pow2(x, d):
    assert d & (d - 1) == 0
    return jnp.right_shift(x, d.bit_length() - 1)


def _rms(x, g):
    return x * lax.rsqrt(jnp.mean(x * x, axis=-1, keepdims=True) + RMS_EPS) * g


def _sigmoid(x):
    return 1.0 / (1.0 + jnp.exp(-x))


def _softplus(x):
    return jnp.maximum(x, 0.0) + jnp.log(1.0 + jnp.exp(-jnp.abs(x)))


def _prep_kernel(x_ref, shift_ref, pool_ref, *rest, nb, tt, pos0, n_prev):
    rest = list(rest)
    pool_prev_ref = rest.pop(0) if n_prev else None
    (n1_ref, win_ref, mu_ref, dec0_ref, wdec_ref, a0_ref, wa_ref, wg_ref, kk_ref, ka_ref, ones_ref,
     wpool_ref, pscale_ref, wbup_ref,
     sc_ref, pg_ref,
     shift_out_ref, pool_out_ref, cz_ref, cu_ref) = rest
    if n_prev:
        pool_out_ref[0:n_prev] = pool_prev_ref[...]
    tm = tt * nb
    cz = cz_ref.shape[0]
    cu = cu_ref.shape[0]
    t = pl.program_id(1)
    hist = cu // nb if nb == 1 else pool_ref.shape[0]

    def put(ref, col, val):
        cols = slice(col, col + val.shape[1])
        if nb == 1:
            ref[0, :, cols] = val
        else:
            for i in range(tt):
                ref[:, i, cols] = val[i * nb:(i + 1) * nb]

    @pl.when(t == 0)
    def _():
        cz_ref[...] = shift_ref[...].reshape(cz_ref.shape)
        past = pool_ref[...].reshape(hist * nb, D_POOL)
        if hist * nb < cu:
            past = jnp.concatenate([jnp.zeros((cu - hist * nb, D_POOL), F32), past], axis=0)
        cu_ref[...] = past

    if nb == 1:
        x = x_ref[...].reshape(tm, D_MODEL)
    else:
        x = jnp.concatenate([x_ref[:, i, :] for i in range(tt)], axis=0)
    hb = _rms(x, n1_ref[...]).astype(BF16)

    c0 = D_SHIFT + D_POOL
    z_rw = jnp.dot(hb, win_ref[:, 0:D_SHIFT], preferred_element_type=F32)
    u = jnp.dot(hb, win_ref[:, D_SHIFT:c0], preferred_element_type=F32)
    z_ga = jnp.dot(hb, win_ref[:, c0:c0 + D_MODEL], preferred_element_type=F32)
    z_gb = jnp.dot(hb, win_ref[:, c0 + D_MODEL:c0 + 2 * D_MODEL], preferred_element_type=F32)

    ext = jnp.concatenate([cz_ref[...], z_rw], axis=0)
    z_prev = pltpu.roll(ext, nb, 0)[cz:]
    zs = z_rw + (z_prev - z_rw) * mu_ref[...]
    new_cz = z_rw[tm - cz:]
    cz_ref[...] = new_cz
    shift_out_ref[...] = new_cz.reshape(shift_out_ref.shape)

    r = zs[:, 0:D_A]
    k = zs[:, D_A:2 * D_A]
    v = zs[:, 2 * D_A:3 * D_A]
    lwa = zs[:, 3 * D_A:3 * D_A + LORA_WA]
    lg = zs[:, 3 * D_A + LORA_WA:D_SHIFT]
    w_log = -_softplus(-(dec0_ref[...] + _dot(jnp.tanh(lwa), wdec_ref[...]))) - 0.5
    a_in = _sigmoid(a0_ref[...] + _dot(lwa, wa_ref[...]))
    kk = k * kk_ref[...]
    kk = kk / jnp.maximum(jnp.sqrt(_seg_sum(kk * kk, ones_ref[...])), L2_EPS)
    put(sc_ref, SC_R, r)
    put(sc_ref, SC_LW, -jnp.exp(w_log))
    put(sc_ref, SC_K, k * (1.0 + (a_in - 1.0) * ka_ref[...]))
    put(sc_ref, SC_V, v)
    put(sc_ref, SC_A, -kk)
    put(sc_ref, SC_B, kk * a_in)
    put(pg_ref, PG_G, _dot(_sigmoid(lg), wg_ref[...]))

    extu = jnp.concatenate([cu_ref[...], u], axis=0)
    new_cu = extu[tm:]
    cu_ref[...] = new_cu
    pool_out_ref[n_prev] = new_cu[cu - hist * nb:].reshape(pool_out_ref.shape[1:])
    row = lax.broadcasted_iota(jnp.int32, (tm, 1), 0)
    step = _div_pow2(row, nb)
    pos = pos0 + t * tt + step
    parts = []
    for gi, win in enumerate(POOL_WINDOWS):
        sl = slice(gi * POOL_GD, (gi + 1) * POOL_GD)
        s = extu[:, sl]
        span = 1
        while span < win:
            s = s + pltpu.roll(s, span * nb, 0)
            span *= 2
        cnt = jnp.minimum(pos + 1, win).astype(F32)
        p = s[cu:] / cnt - u[:, sl]
        parts.append(_dot(p, wpool_ref[gi]))
    pooled = jnp.concatenate(parts, axis=-1) * pscale_ref[...]
    b_out = _dot(pooled, wbup_ref[...])

    put(pg_ref, PG_GA, _sigmoid(z_ga))
    put(pg_ref, PG_GBB, _sigmoid(z_gb) * b_out)


def _layer_spec(w, l):
    li = min(l, w.shape[0] - 1)
    nd = w.ndim - 1
    return pl.BlockSpec((None,) + w.shape[1:], lambda *_: (li,) + (0,) * nd, pipeline_mode=pl.Buffered(1))


def _prep_call(x, shift_in, pool_in, pool_prev, wts, l, *, nb, tt, pos0, grid, row_block, row_map,
               shift_shape, shift_in_map, shift_out_map, pool_shape, pool_block, pool_in_spec, pool_stack_map):
    lead = x.shape[:-1]
    n_prev = 0 if pool_prev is None else pool_prev.shape[0]
    outs_rows = lambda w: jax.ShapeDtypeStruct(lead + (w,), F32)
    rb = lambda w: pl.BlockSpec(row_block + (w,), row_map)
    cz = max(SUBLANES, nb)
    cu = POOL_CARRY * nb
    shift_block = (1, cz, D_SHIFT)
    stacked = lambda n: pl.BlockSpec((n,) + pool_block, pool_stack_map)
    in_specs = ([rb(D_MODEL), pl.BlockSpec(shift_block, shift_in_map), pool_in_spec]
                + ([stacked(n_prev)] if n_prev else []) + [_layer_spec(w, l) for w in wts])
    out_shape = [outs_rows(SC_WIDTH), outs_rows(PG_WIDTH),
                 jax.ShapeDtypeStruct(shift_shape, F32), jax.ShapeDtypeStruct((n_prev + 1,) + pool_shape, F32)]
    out_specs = [rb(SC_WIDTH), rb(PG_WIDTH), pl.BlockSpec(shift_block, shift_out_map), stacked(n_prev + 1)]
    return pl.pallas_call(
        functools.partial(_prep_kernel, nb=nb, tt=tt, pos0=pos0, n_prev=n_prev),
        grid=grid, in_specs=in_specs, out_specs=out_specs, out_shape=out_shape,
        scratch_shapes=[pltpu.VMEM((cz, D_SHIFT), F32), pltpu.VMEM((cu, D_POOL), F32)],
        compiler_params=pltpu.CompilerParams(dimension_semantics=("arbitrary", "arbitrary"),
                                             vmem_limit_bytes=VMEM_LIMIT),
        name="prep",
    )(x, shift_in, pool_in, *([pool_prev] if n_prev else []), *wts)


def _scan_masks(steps):
    assert CHUNK == HEAD_DIM
    ri = lax.broadcasted_iota(jnp.int32, (CHUNK, LANES), 0)
    ci = jnp.bitwise_and(lax.broadcasted_iota(jnp.int32, (CHUNK, LANES), 1), CHUNK - 1)
    blk = lambda s: _div_pow2(ri, s) == _div_pow2(ci, s)
    same_seq = blk(steps)
    strict = same_seq & (ri > ci)
    incl = same_seq & (ri >= ci)
    levels = []
    size = SUBLANES
    while size < steps:
        size *= 2
        levels.append(strict & blk(size) & ~blk(size // 2))
    base = strict & blk(SUBLANES)
    eye = ri == ci
    return same_seq, strict, incl, base, levels, eye


def _bdot(a, b):
    return lax.dot_general(a.astype(BF16), b.astype(BF16), (((2,), (1,)), ((0,), (0,))),
                           preferred_element_type=F32)


def _bdot_nt(a, b):
    return lax.dot_general(a.astype(BF16), b.astype(BF16), (((2,), (2,)), ((0,), (0,))),
                           preferred_element_type=F32)


def _bdot_tn(a, b):
    return lax.dot_general(a.astype(BF16), b.astype(BF16), (((1,), (1,)), ((0,), (0,))),
                           preferred_element_type=F32)


def _head_stack(x):
    first = lax.broadcasted_iota(jnp.int32, (1, 1, LANES), 2) < HEAD_DIM
    return jnp.concatenate([jnp.where(first, x, 0.0), jnp.where(first, 0.0, x)], axis=1)


def _hdot(a, b):
    return _bdot(a, _head_stack(b))


def _tri_inverse(a_strict, base, levels, eye):
    ad = jnp.where(base, a_strict, 0.0)
    p2 = _hdot(ad, ad)
    t = jnp.where(eye, 1.0, ad)
    t = t + _hdot(t, p2)
    t = t + _hdot(t, _hdot(p2, p2))
    for lvl in levels:
        t = t + _hdot(t, _hdot(jnp.where(lvl, a_strict, 0.0), t))
    return t


def _scan_kernel(sc_ref, h0_ref, *rest, steps, n_chunks, n_prev):
    y_ref, hout_ref, h_ref = rest[-3:]
    if n_prev:
        hout_ref[0:n_prev] = rest[0][...]
    nseq = CHUNK // steps
    nst = h_ref.shape[0]
    t = pl.program_id(1)

    @pl.when(t == 0)
    def _():
        zero = jnp.zeros((HEAD_DIM, HEAD_DIM), F32)
        for s in range(nst):
            for p in range(PAIRS):
                top = jnp.concatenate([h0_ref[0, s, p, 0], zero], axis=1)
                bot = jnp.concatenate([zero, h0_ref[0, s, p, 1]], axis=1)
                h_ref[s, p] = jnp.concatenate([top, bot], axis=0)

    _, strict, incl, base, levels, eye = _scan_masks(steps)
    hi = lax.broadcasted_iota(jnp.int32, (LANES, LANES), 0)
    hj = lax.broadcasted_iota(jnp.int32, (LANES, LANES), 1)
    same_head = _div_pow2(hi, HEAD_DIM) == _div_pow2(hj, HEAD_DIM)
    rows = n_chunks * CHUNK

    sc = sc_ref[...].reshape(rows, SC_WIDTH)
    r, lw, k, v, a, b = (sc[:, c0:c0 + D_A] for c0 in (SC_R, SC_LW, SC_K, SC_V, SC_A, SC_B))
    pos = jnp.bitwise_and(lax.broadcasted_iota(jnp.int32, (rows, 1), 0), steps - 1)
    cs = lw
    span = 1
    while span < steps:
        cs = cs + jnp.where(pos >= span, pltpu.roll(cs, span, 0), 0.0)
        span *= 2
    cl = jnp.broadcast_to(cs.reshape(rows // steps, steps, D_A)[:, steps - 1:steps, :],
                          (rows // steps, steps, D_A)).reshape(rows, D_A)
    e_cs = jnp.exp(cs)
    e_ncs = jnp.exp(-cs)
    e_cl = jnp.exp(cl - cs)
    gam = jnp.exp(cl)

    def inst(x):
        x3 = x.reshape(n_chunks, CHUNK, D_A)
        return jnp.concatenate([x3[:, :, p * LANES:(p + 1) * LANES] for p in range(PAIRS)], axis=0)

    at = inst(a * jnp.exp(cs - lw))
    rt = inst(r * e_cs)
    bh = inst(b * e_cl)
    kh = inst(k * e_cl)
    vv = inst(v)
    aa = _bdot_nt(jnp.concatenate([at, rt], axis=1),
                  jnp.concatenate([_head_stack(inst(b * e_ncs)), _head_stack(inst(k * e_ncs))], axis=1))
    a_ab = jnp.where(strict, aa[:, :CHUNK, :LANES], 0.0)
    a_ak = jnp.where(strict, aa[:, :CHUNK, LANES:], 0.0)
    a_rb = jnp.where(incl, aa[:, CHUNK:, :LANES], 0.0)
    a_rk = jnp.where(incl, aa[:, CHUNK:, LANES:], 0.0)
    tinv = _tri_inverse(a_ab, base, levels, eye)
    akv = _hdot(jnp.concatenate([a_ak, a_rk], axis=1), vv)
    tx = _bdot(tinv, jnp.concatenate([_head_stack(at), _head_stack(akv[:, :CHUNK])], axis=2))
    ah = tx[:, :, :LANES]
    w1 = tx[:, :, LANES:]
    rx = _bdot(a_rb, jnp.concatenate([_head_stack(ah), _head_stack(w1)], axis=2))
    rh = rt + rx[:, :, :LANES]
    y1 = rx[:, :, LANES:] + akv[:, CHUNK:]

    ng = PAIRS * n_chunks
    bf16_rows = 2 * SUBLANES

    def seqsplit(x):
        parts = [x[:, s * steps:(s + 1) * steps] for s in range(nseq)]
        if steps < bf16_rows:
            pad = jnp.zeros((x.shape[0], bf16_rows - steps, x.shape[2]), F32)
            parts = [jnp.concatenate([q, pad], axis=1) for q in parts]
        return jnp.concatenate(parts, axis=0)

    bhs = seqsplit(bh)
    mp = jnp.where(same_head, _bdot_tn(seqsplit(ah), bhs), 0.0)
    np_ = jnp.where(same_head, _bdot_tn(jnp.concatenate([seqsplit(w1), seqsplit(vv)], axis=1),
                                        jnp.concatenate([bhs, seqsplit(kh)], axis=1)), 0.0)
    gam_i = inst(gam)
    gam_m = jnp.concatenate([gam_i[:, s * steps:s * steps + 1] for s in range(nseq)], axis=0)
    order = [(s, p, c) for s in range(nseq) for p in range(PAIRS) for c in range(n_chunks)]
    n_groups = nst // nseq
    cpg = n_chunks // n_groups
    lanes_of = [(q, s, p) for q in range(n_groups) for s in range(nseq) for p in range(PAIRS)]
    take = lambda x, pos: jnp.concatenate(
        [x[s * ng + p * n_chunks + q * cpg + pos][None] for (q, s, p) in lanes_of], axis=0)
    cur = jnp.concatenate([h_ref[q * nseq + s, p][None] for (q, s, p) in lanes_of], axis=0)
    starts = []
    for pos in range(cpg):
        starts.append(cur)
        cur = cur * take(gam_m, pos) + _bdot(cur, take(mp, pos)) + take(np_, pos)
    slot = {key: i for i, key in enumerate(lanes_of)}
    s_start = jnp.concatenate(
        [starts[c % cpg][slot[(c // cpg, s, p)]][None] for (s, p, c) in order], axis=0)
    finals = {(q * nseq + s, p): cur[i] for i, (q, s, p) in enumerate(lanes_of)}
    y = (_bdot_nt(seqsplit(rh), s_start) + seqsplit(y1))[:, :steps]
    for i, (s, p, c) in enumerate(order):
        r0 = c * CHUNK + s * steps
        if len(y_ref.shape) == 3:
            per = y_ref.shape[1]
            y_ref[r0 // per, r0 % per:r0 % per + steps, p * LANES:(p + 1) * LANES] = y[i]
        else:
            y_ref[r0:r0 + steps, p * LANES:(p + 1) * LANES] = y[i]
    for (s, p), h in finals.items():
        h_ref[s, p] = h
        hout_ref[n_prev, 0, s, p, 0] = h[:HEAD_DIM, :HEAD_DIM]
        hout_ref[n_prev, 0, s, p, 1] = h[HEAD_DIM:, HEAD_DIM:]


def _scan_call(sc, h0, prev, l, *, steps, n_chunks, grid, row_block, row_map):
    nst = h0.shape[2]
    li = min(l, h0.shape[0] - 1)
    n_prev = 0 if prev is None else prev.shape[0]
    st_block = (1, nst, PAIRS, 2, HEAD_DIM, HEAD_DIM)
    rb = lambda w: pl.BlockSpec(row_block + (w,), row_map)
    hb = pl.BlockSpec((None,) + st_block, lambda b, t: (li, b, 0, 0, 0, 0, 0))
    stacked = lambda n: pl.BlockSpec((n,) + st_block, lambda b, t: (0, b, 0, 0, 0, 0, 0))
    return pl.pallas_call(
        functools.partial(_scan_kernel, steps=steps, n_chunks=n_chunks, n_prev=n_prev),
        grid=grid, in_specs=[rb(SC_WIDTH), hb] + ([stacked(n_prev)] if n_prev else []),
        out_specs=[rb(D_A), stacked(n_prev + 1)],
        out_shape=[jax.ShapeDtypeStruct(sc.shape[:-1] + (D_A,), F32),
                   jax.ShapeDtypeStruct((n_prev + 1,) + h0.shape[1:], F32)],
        scratch_shapes=[pltpu.VMEM((nst, PAIRS, LANES, LANES), F32)],
        compiler_params=pltpu.CompilerParams(dimension_semantics=("arbitrary", "arbitrary"),
                                             vmem_limit_bytes=VMEM_LIMIT),
        name="scan",
    )(sc, h0, *([prev] if n_prev else []))


def _post_kernel(y_ref, r_ref, k_ref, v_ref, pg_ref, x_ref, ones_ref, rk_ref, lng_ref,
                 lnb_ref, waup_ref, wo_ref, n2_ref, w1_ref, w2_ref, fn_ref, o_ref, *, final):
    ones = ones_ref[...]
    hm = y_ref.shape[0] // 2

    def head(rows):
        y = y_ref[rows, :]
        mu = _seg_sum(y, ones) * (1.0 / HEAD_DIM)
        d = y - mu
        var = _seg_sum(d * d, ones) * (1.0 / HEAD_DIM)
        yn = d * lax.rsqrt(var + GN_EPS) * lng_ref[...] + lnb_ref[...]
        bonus = _seg_sum(r_ref[rows, :] * k_ref[rows, :] * rk_ref[...], ones) * v_ref[rows, :]
        return (yn + bonus) * pg_ref[rows, PG_G:PG_GA]

    def mix(pre, rows):
        a_out = _dot(pre, waup_ref[...])
        merged = pg_ref[rows, PG_GA:PG_GBB] * a_out + pg_ref[rows, PG_GBB:PG_WIDTH]
        x1 = x_ref[rows, :] + _dot(merged, wo_ref[...])
        return x1, _rms(x1, n2_ref[...]).astype(BF16)

    def ffn(x1, hb, rows):
        x2 = x1
        for c in range(0, D_FF, FF_CHUNK):
            f = jnp.maximum(jnp.dot(hb, w1_ref[:, c:c + FF_CHUNK], preferred_element_type=F32), 0.0)
            x2 = x2 + _dot(f * f, w2_ref[c:c + FF_CHUNK, :])
        o_ref[rows, :] = _rms(x2, fn_ref[...]) if final else x2

    ra, rb_ = slice(0, hm), slice(hm, 2 * hm)
    pre_a = head(ra)
    xa, ha = mix(pre_a, ra)
    pre_b = head(rb_)
    ffn(xa, ha, ra)
    xb, hb_ = mix(pre_b, rb_)
    ffn(xb, hb_, rb_)


def _post_call(y, sc, pg, x, wts, l, *, final, tm):
    n = y.shape[0]
    rb = lambda a: pl.BlockSpec((tm, a.shape[-1]), lambda i: (i, 0))
    section = lambda c0: pl.BlockSpec((tm, D_A), lambda i: (i, c0 // D_A))
    return pl.pallas_call(
        functools.partial(_post_kernel, final=final),
        grid=(n // tm,),
        in_specs=[rb(y), section(SC_R), section(SC_K), section(SC_V), rb(pg), rb(x)]
                 + [_layer_spec(w, l) for w in wts],
        out_specs=pl.BlockSpec((tm, D_MODEL), lambda i: (i, 0)),
        out_shape=jax.ShapeDtypeStruct((n, D_MODEL), F32),
        compiler_params=pltpu.CompilerParams(dimension_semantics=("arbitrary",),
                                             vmem_limit_bytes=VMEM_LIMIT),
        name="post",
    )(y, sc, sc, sc, pg, x, *wts)


def _stacked_weights(w_in, mu_shift, decay0, w_decay2, a0, w_a2, w_g2, k_k, k_a, r_k, ln_x_g, ln_x_b,
                     w_a_up, w_pool, pool_scale, w_b_up, w_o, norm1_g, norm2_g, w_ff1, w_ff2, final_norm_g):
    depth = w_in.shape[0]
    row = lambda a: a.reshape(a.shape[0], 1, -1)
    zpad = jnp.zeros((depth, LORA_WA // 2, D_A), F32)
    seg = jnp.arange(MXU_DIM) // HEAD_DIM
    ones_bd = (seg[:, None] == seg[None, :]).astype(BF16)[None]
    prep = [row(norm1_g), w_in.astype(BF16), row(mu_shift), row(decay0),
            jnp.concatenate([w_decay2, zpad], axis=1).astype(BF16), row(a0),
            jnp.concatenate([zpad, w_a2], axis=1).astype(BF16), w_g2.astype(BF16),
            row(k_k), row(k_a), ones_bd, w_pool.astype(BF16), row(pool_scale), w_b_up.astype(BF16)]
    post = [ones_bd, row(r_k), row(ln_x_g), row(ln_x_b), w_a_up.astype(BF16), w_o.astype(BF16),
            row(norm2_g), w_ff1.astype(BF16), w_ff2.astype(BF16), final_norm_g.reshape(1, 1, -1)]
    return prep, post


def kernel(x_prompt, x_sample, state_shift, state_pool, state_wkv, norm1_g, w_in, mu_shift, decay0,
           w_decay2, a0, w_a2, w_g2, k_k, k_a, r_k, ln_x_g, ln_x_b, w_a_up, w_pool, pool_scale, w_b_up,
           w_o, norm2_g, w_ff1, w_ff2, final_norm_g):
    depth = w_in.shape[0]
    bp, tp, _ = x_prompt.shape
    bs, ts, _ = x_sample.shape
    tt_p = 512
    nb_s = 32
    ts_p = 256
    nb_scan = 4
    tm_post = 512
    nc_s = 2
    nseq = CHUNK // ts
    hist = state_pool.shape[2]
    prep_w, post_w = _stacked_weights(
        w_in, mu_shift, decay0, w_decay2, a0, w_a2, w_g2, k_k, k_a, r_k, ln_x_g, ln_x_b, w_a_up, w_pool,
        pool_scale, w_b_up, w_o, norm1_g, norm2_g, w_ff1, w_ff2, final_norm_g)

    p_shift0 = jnp.zeros((bp, SUBLANES, D_SHIFT), F32)
    p_pool0 = jnp.zeros((bp, POOL_CARRY, D_POOL), F32)
    p_wkv0 = jnp.zeros((1, bp // nb_scan, nb_scan, PAIRS, 2, HEAD_DIM, HEAD_DIM), F32)
    s_pool0 = jnp.swapaxes(state_pool, 1, 2)
    s_wkv0 = state_wkv.reshape(depth, bs // (nseq * nc_s), nseq * nc_s, PAIRS, 2, HEAD_DIM, HEAD_DIM)

    xp = x_prompt
    xs = x_sample.reshape(bs * ts, D_MODEL)
    outs = {k: [] for k in ("p_shift", "s_shift")}
    p_pool = p_wkv = s_pool = s_wkv = None
    for l in range(depth):
        final = l == depth - 1

        res = _prep_call(
            xp, p_shift0, p_pool0, p_pool, prep_w, l, nb=1, tt=tt_p, pos0=0, grid=(bp, tp // tt_p),
            row_block=(1, tt_p), row_map=lambda b, t: (b, t, 0),
            shift_shape=p_shift0.shape, shift_in_map=lambda b, t: (b, 0, 0),
            shift_out_map=lambda b, t: (b, 0, 0), pool_shape=p_pool0.shape,
            pool_block=(1, POOL_CARRY, D_POOL),
            pool_in_spec=pl.BlockSpec((1, POOL_CARRY, D_POOL), lambda b, t: (b, 0, 0)),
            pool_stack_map=lambda b, t: (0, b, 0, 0))
        sc, pg, sh, p_pool = res
        outs["p_shift"].append(sh[:, SUBLANES - 1])
        y, p_wkv = _scan_call(
            sc, p_wkv0, p_wkv, l, steps=CHUNK, n_chunks=nb_scan * ts_p // CHUNK,
            grid=(bp // nb_scan, tp // ts_p), row_block=(nb_scan, ts_p), row_map=lambda b, t: (b, t, 0))
        flat = lambda z: z.reshape(bp * tp, z.shape[-1])
        xp = _post_call(flat(y), flat(sc), flat(pg), flat(xp), post_w, l, final=final,
                        tm=tm_post).reshape(bp, tp, D_MODEL)

        res = _prep_call(
            xs.reshape(bs, ts, D_MODEL), state_shift, s_pool0, s_pool, prep_w, l, nb=nb_s, tt=ts,
            pos0=PAST_LEN, grid=(bs // nb_s, 1), row_block=(nb_s, ts), row_map=lambda b, t: (b, 0, 0),
            shift_shape=(1, bs, D_SHIFT), shift_in_map=lambda b, t, l=l: (l, b, 0),
            shift_out_map=lambda b, t: (0, b, 0), pool_shape=(hist, bs, D_POOL),
            pool_block=(hist, nb_s, D_POOL),
            pool_in_spec=pl.BlockSpec((None, hist, nb_s, D_POOL), lambda b, t, l=l: (l, 0, b, 0)),
            pool_stack_map=lambda b, t: (0, 0, b, 0))
        sc, pg = (z.reshape(bs * ts, z.shape[-1]) for z in res[:2])
        outs["s_shift"].append(res[2][0])
        s_pool = res[3]
        y, s_wkv = _scan_call(
            sc, s_wkv0, s_wkv, l, steps=ts, n_chunks=nc_s,
            grid=(bs // (nseq * nc_s), 1), row_block=(nc_s * CHUNK,), row_map=lambda b, t: (b, 0))
        xs = _post_call(y, sc, pg, xs, post_w, l, final=final, tm=tm_post)

    st = lambda key: jnp.stack(outs[key])
    wkv = lambda h, n: h.reshape(depth, n, N_HEADS, HEAD_DIM, HEAD_DIM)
    return (xp, xs.reshape(bs, ts, D_MODEL), st("p_shift"), p_pool[:, :, 1:], wkv(p_wkv, bp),
            st("s_shift"), jnp.swapaxes(s_pool, 1, 2), wkv(s_wkv, bs))
```

```python
import functools

import jax
import jax.numpy as jnp
from jax import lax
from jax.experimental import pallas as pl
from jax.experimental.pallas import tpu as pltpu

F32 = jnp.float32
BF16 = jnp.bfloat16

D_MODEL = 1024
HEAD_DIM = 64
D_A = 512
N_HEADS = 8
LORA_WA = 128
LORA_G = 128
D_SHIFT = 3 * D_A + LORA_WA + LORA_G
D_POOL = 512
POOL_WINDOWS = (2, 4, 8, 16)
POOL_GD = 128
POOL_CARRY = 16
D_FF = 4096
PAST_LEN = 16384
RMS_EPS = 1e-6
GN_EPS = HEAD_DIM * 1e-5
L2_EPS = 1e-12

LANES = 128
SUBLANES = 8
MXU_DIM = 256
CHUNK = 64
FF_CHUNK = 1024
PAIRS = D_A // LANES
SC_R, SC_LW, SC_K, SC_V, SC_A, SC_B, SC_WIDTH = (i * D_A for i in range(7))
PG_G, PG_GA, PG_GBB, PG_WIDTH = 0, D_A, D_A + D_MODEL, D_A + 2 * D_MODEL
VMEM_LIMIT = 56 * 1024 * 1024


def _dot(a, b):
    return jnp.dot(a.astype(BF16), b.astype(BF16), preferred_element_type=F32)


def _seg_sum(x, ones_bd):
    hi = x.astype(BF16)
    lo = (x - hi.astype(F32)).astype(BF16)
    w = ones_bd.shape[0]
    d = lambda p: jnp.concatenate(
        [jnp.dot(p[:, i:i + w], ones_bd, preferred_element_type=F32) for i in range(0, x.shape[1], w)],
        axis=1)
    return d(hi) + d(lo)


def _div_pow2(x, d):
    assert d & (d - 1) == 0
    return jnp.right_shift(x, d.bit_length() - 1)


def _rms(x, g):
    return x * lax.rsqrt(jnp.mean(x * x, axis=-1, keepdims=True) + RMS_EPS) * g


def _sigmoid(x):
    return 1.0 / (1.0 + jnp.exp(-x))


def _softplus(x):
    return jnp.maximum(x, 0.0) + jnp.log(1.0 + jnp.exp(-jnp.abs(x)))


def _prep_kernel(x_ref, shift_ref, pool_ref, *rest, nb, tt, pos0, n_prev):
    rest = list(rest)
    pool_prev_ref = rest.pop(0) if n_prev else None
    (n1_ref, win_ref, mu_ref, dec0_ref, wdec_ref, a0_ref, wa_ref, wg_ref, kk_ref, ka_ref, ones_ref,
     wpool_ref, pscale_ref, wbup_ref,
     sc_ref, pg_ref,
     shift_out_ref, pool_out_ref, cz_ref, cu_ref) = rest
    if n_prev:
        pool_out_ref[0:n_prev] = pool_prev_ref[...]
    tm = tt * nb
    cz = cz_ref.shape[0]
    cu = cu_ref.shape[0]
    t = pl.program_id(1)
    hist = cu // nb if nb == 1 else pool_ref.shape[0]

    def put(ref, col, val):
        cols = slice(col, col + val.shape[1])
        if nb == 1:
            ref[0, :, cols] = val
        else:
            for i in range(tt):
                ref[:, i, cols] = val[i * nb:(i + 1) * nb]

    @pl.when(t == 0)
    def _():
        cz_ref[...] = shift_ref[...].reshape(cz_ref.shape)
        past = pool_ref[...].reshape(hist * nb, D_POOL)
        if hist * nb < cu:
            past = jnp.concatenate([jnp.zeros((cu - hist * nb, D_POOL), F32), past], axis=0)
        cu_ref[...] = past

    if nb == 1:
        x = x_ref[...].reshape(tm, D_MODEL)
    else:
        x = jnp.concatenate([x_ref[:, i, :] for i in range(tt)], axis=0)
    hb = _rms(x, n1_ref[...]).astype(BF16)

    c0 = D_SHIFT + D_POOL
    z_rw = jnp.dot(hb, win_ref[:, 0:D_SHIFT], preferred_element_type=F32)
    u = jnp.dot(hb, win_ref[:, D_SHIFT:c0], preferred_element_type=F32)
    z_ga = jnp.dot(hb, win_ref[:, c0:c0 + D_MODEL], preferred_element_type=F32)
    z_gb = jnp.dot(hb, win_ref[:, c0 + D_MODEL:c0 + 2 * D_MODEL], preferred_element_type=F32)

    ext = jnp.concatenate([cz_ref[...], z_rw], axis=0)
    z_prev = pltpu.roll(ext, nb, 0)[cz:]
    zs = z_rw + (z_prev - z_rw) * mu_ref[...]
    new_cz = z_rw[tm - cz:]
    cz_ref[...] = new_cz
    shift_out_ref[...] = new_cz.reshape(shift_out_ref.shape)

    r = zs[:, 0:D_A]
    k = zs[:, D_A:2 * D_A]
    v = zs[:, 2 * D_A:3 * D_A]
    lwa = zs[:, 3 * D_A:3 * D_A + LORA_WA]
    lg = zs[:, 3 * D_A + LORA_WA:D_SHIFT]
    w_log = -_softplus(-(dec0_ref[...] + _dot(jnp.tanh(lwa), wdec_ref[...]))) - 0.5
    a_in = _sigmoid(a0_ref[...] + _dot(lwa, wa_ref[...]))
    kk = k * kk_ref[...]
    kk = kk / jnp.maximum(jnp.sqrt(_seg_sum(kk * kk, ones_ref[...])), L2_EPS)
    put(sc_ref, SC_R, r)
    put(sc_ref, SC_LW, -jnp.exp(w_log))
    put(sc_ref, SC_K, k * (1.0 + (a_in - 1.0) * ka_ref[...]))
    put(sc_ref, SC_V, v)
    put(sc_ref, SC_A, -kk)
    put(sc_ref, SC_B, kk * a_in)
    put(pg_ref, PG_G, _dot(_sigmoid(lg), wg_ref[...]))

    extu = jnp.concatenate([cu_ref[...], u], axis=0)
    new_cu = extu[tm:]
    cu_ref[...] = new_cu
    pool_out_ref[n_prev] = new_cu[cu - hist * nb:].reshape(pool_out_ref.shape[1:])
    row = lax.broadcasted_iota(jnp.int32, (tm, 1), 0)
    step = _div_pow2(row, nb)
    pos = pos0 + t * tt + step
    parts = []
    for gi, win in enumerate(POOL_WINDOWS):
        sl = slice(gi * POOL_GD, (gi + 1) * POOL_GD)
        s = extu[:, sl]
        span = 1
        while span < win:
            s = s + pltpu.roll(s, span * nb, 0)
            span *= 2
        cnt = jnp.minimum(pos + 1, win).astype(F32)
        p = s[cu:] / cnt - u[:, sl]
        parts.append(_dot(p, wpool_ref[gi]))
    pooled = jnp.concatenate(parts, axis=-1) * pscale_ref[...]
    b_out = _dot(pooled, wbup_ref[...])

    put(pg_ref, PG_GA, _sigmoid(z_ga))
    put(pg_ref, PG_GBB, _sigmoid(z_gb) * b_out)


def _layer_spec(w, l):
    li = min(l, w.shape[0] - 1)
    nd = w.ndim - 1
    return pl.BlockSpec((None,) + w.shape[1:], lambda *_: (li,) + (0,) * nd, pipeline_mode=pl.Buffered(1))


def _prep_call(x, shift_in, pool_in, pool_prev, wts, l, *, nb, tt, pos0, grid, row_block, row_map,
               shift_shape, shift_in_map, shift_out_map, pool_shape, pool_block, pool_in_spec, pool_stack_map):
    lead = x.shape[:-1]
    n_prev = 0 if pool_prev is None else pool_prev.shape[0]
    outs_rows = lambda w: jax.ShapeDtypeStruct(lead + (w,), F32)
    rb = lambda w: pl.BlockSpec(row_block + (w,), row_map)
    cz = max(SUBLANES, nb)
    cu = POOL_CARRY * nb
    shift_block = (1, cz, D_SHIFT)
    stacked = lambda n: pl.BlockSpec((n,) + pool_block, pool_stack_map)
    in_specs = ([rb(D_MODEL), pl.BlockSpec(shift_block, shift_in_map), pool_in_spec]
                + ([stacked(n_prev)] if n_prev else []) + [_layer_spec(w, l) for w in wts])
    out_shape = [outs_rows(SC_WIDTH), outs_rows(PG_WIDTH),
                 jax.ShapeDtypeStruct(shift_shape, F32), jax.ShapeDtypeStruct((n_prev + 1,) + pool_shape, F32)]
    out_specs = [rb(SC_WIDTH), rb(PG_WIDTH), pl.BlockSpec(shift_block, shift_out_map), stacked(n_prev + 1)]
    return pl.pallas_call(
        functools.partial(_prep_kernel, nb=nb, tt=tt, pos0=pos0, n_prev=n_prev),
        grid=grid, in_specs=in_specs, out_specs=out_specs, out_shape=out_shape,
        scratch_shapes=[pltpu.VMEM((cz, D_SHIFT), F32), pltpu.VMEM((cu, D_POOL), F32)],
        compiler_params=pltpu.CompilerParams(dimension_semantics=("arbitrary", "arbitrary"),
                                             vmem_limit_bytes=VMEM_LIMIT),
        name="prep",
    )(x, shift_in, pool_in, *([pool_prev] if n_prev else []), *wts)


def _scan_masks(steps):
    assert CHUNK == HEAD_DIM
    ri = lax.broadcasted_iota(jnp.int32, (CHUNK, LANES), 0)
    ci = jnp.bitwise_and(lax.broadcasted_iota(jnp.int32, (CHUNK, LANES), 1), CHUNK - 1)
    blk = lambda s: _div_pow2(ri, s) == _div_pow2(ci, s)
    same_seq = blk(steps)
    strict = same_seq & (ri > ci)
    incl = same_seq & (ri >= ci)
    levels = []
    size = SUBLANES
    while size < steps:
        size *= 2
        levels.append(strict & blk(size) & ~blk(size // 2))
    base = strict & blk(SUBLANES)
    eye = ri == ci
    return same_seq, strict, incl, base, levels, eye


def _bdot(a, b):
    return lax.dot_general(a.astype(BF16), b.astype(BF16), (((2,), (1,)), ((0,), (0,))),
                           preferred_element_type=F32)


def _bdot_nt(a, b):
    return lax.dot_general(a.astype(BF16), b.astype(BF16), (((2,), (2,)), ((0,), (0,))),
                           preferred_element_type=F32)


def _bdot_tn(a, b):
    return lax.dot_general(a.astype(BF16), b.astype(BF16), (((1,), (1,)), ((0,), (0,))),
                           preferred_element_type=F32)


def _head_stack(x):
    first = lax.broadcasted_iota(jnp.int32, (1, 1, LANES), 2) < HEAD_DIM
    return jnp.concatenate([jnp.where(first, x, 0.0), jnp.where(first, 0.0, x)], axis=1)


def _hdot(a, b):
    return _bdot(a, _head_stack(b))


def _tri_inverse(a_strict, base, levels, eye):
    ad = jnp.where(base, a_strict, 0.0)
    p2 = _hdot(ad, ad)
    t = jnp.where(eye, 1.0, ad)
    t = t + _hdot(t, p2)
    t = t + _hdot(t, _hdot(p2, p2))
    for lvl in levels:
        t = t + _hdot(t, _hdot(jnp.where(lvl, a_strict, 0.0), t))
    return t


def _scan_kernel(sc_ref, h0_ref, *rest, steps, n_chunks, n_prev):
    y_ref, hout_ref, h_ref = rest[-3:]
    if n_prev:
        hout_ref[0:n_prev] = rest[0][...]
    nseq = CHUNK // steps
    nst = h_ref.shape[0]
    t = pl.program_id(1)

    @pl.when(t == 0)
    def _():
        zero = jnp.zeros((HEAD_DIM, HEAD_DIM), F32)
        for s in range(nst):
            for p in range(PAIRS):
                top = jnp.concatenate([h0_ref[0, s, p, 0], zero], axis=1)
                bot = jnp.concatenate([zero, h0_ref[0, s, p, 1]], axis=1)
                h_ref[s, p] = jnp.concatenate([top, bot], axis=0)

    _, strict, incl, base, levels, eye = _scan_masks(steps)
    hi = lax.broadcasted_iota(jnp.int32, (LANES, LANES), 0)
    hj = lax.broadcasted_iota(jnp.int32, (LANES, LANES), 1)
    same_head = _div_pow2(hi, HEAD_DIM) == _div_pow2(hj, HEAD_DIM)
    rows = n_chunks * CHUNK

    sc = sc_ref[...].reshape(rows, SC_WIDTH)
    r, lw, k, v, a, b = (sc[:, c0:c0 + D_A] for c0 in (SC_R, SC_LW, SC_K, SC_V, SC_A, SC_B))
    pos = jnp.bitwise_and(lax.broadcasted_iota(jnp.int32, (rows, 1), 0), steps - 1)
    cs = lw
    span = 1
    while span < steps:
        cs = cs + jnp.where(pos >= span, pltpu.roll(cs, span, 0), 0.0)
        span *= 2
    cl = jnp.broadcast_to(cs.reshape(rows // steps, steps, D_A)[:, steps - 1:steps, :],
                          (rows // steps, steps, D_A)).reshape(rows, D_A)
    e_cs = jnp.exp(cs)
    e_ncs = jnp.exp(-cs)
    gam = jnp.exp(cl)
    bt_raw = b * e_ncs
    kt_raw = k * e_ncs

    def inst(x):
        x3 = x.reshape(n_chunks, CHUNK, D_A)
        return jnp.concatenate([x3[:, :, p * LANES:(p + 1) * LANES] for p in range(PAIRS)], axis=0)

    at = inst(a * jnp.exp(cs - lw))
    rt = inst(r * e_cs)
    bh = inst(bt_raw * gam)
    kh = inst(kt_raw * gam)
    vv = inst(v)
    aa = _bdot_nt(jnp.concatenate([at, rt], axis=1),
                  jnp.concatenate([_head_stack(inst(bt_raw)), _head_stack(inst(kt_raw))], axis=1))
    a_ab = jnp.where(strict, aa[:, :CHUNK, :LANES], 0.0)
    a_ak = jnp.where(strict, aa[:, :CHUNK, LANES:], 0.0)
    a_rb = jnp.where(incl, aa[:, CHUNK:, :LANES], 0.0)
    a_rk = jnp.where(incl, aa[:, CHUNK:, LANES:], 0.0)
    tinv = _tri_inverse(a_ab, base, levels, eye)
    akv = _hdot(jnp.concatenate([a_ak, a_rk], axis=1), vv)
    tx = _bdot(tinv, jnp.concatenate([_head_stack(at), _head_stack(akv[:, :CHUNK])], axis=2))
    ah = tx[:, :, :LANES]
    w1 = tx[:, :, LANES:]
    rx = _bdot(a_rb, jnp.concatenate([_head_stack(ah), _head_stack(w1)], axis=2))
    rh = rt + rx[:, :, :LANES]
    y1 = rx[:, :, LANES:] + akv[:, CHUNK:]

    ng = PAIRS * n_chunks
    bf16_rows = 2 * SUBLANES

    def seqsplit(x):
        parts = [x[:, s * steps:(s + 1) * steps] for s in range(nseq)]
        if steps < bf16_rows:
            pad = jnp.zeros((x.shape[0], bf16_rows - steps, x.shape[2]), F32)
            parts = [jnp.concatenate([q, pad], axis=1) for q in parts]
        return jnp.concatenate(parts, axis=0)

    bhs = seqsplit(bh)
    mp = jnp.where(same_head, _bdot_tn(seqsplit(ah), bhs), 0.0)
    np_ = jnp.where(same_head, _bdot_tn(jnp.concatenate([seqsplit(w1), seqsplit(vv)], axis=1),
                                        jnp.concatenate([bhs, seqsplit(kh)], axis=1)), 0.0)
    gam_i = inst(gam)
    gam_m = jnp.concatenate([gam_i[:, s * steps:s * steps + 1] for s in range(nseq)], axis=0)
    order = [(s, p, c) for s in range(nseq) for p in range(PAIRS) for c in range(n_chunks)]
    n_groups = nst // nseq
    cpg = n_chunks // n_groups
    lanes_of = [(q, s, p) for q in range(n_groups) for s in range(nseq) for p in range(PAIRS)]
    take = lambda x, pos: jnp.concatenate(
        [x[s * ng + p * n_chunks + q * cpg + pos][None] for (q, s, p) in lanes_of], axis=0)
    cur = jnp.concatenate([h_ref[q * nseq + s, p][None] for (q, s, p) in lanes_of], axis=0)
    starts = []
    for pos in range(cpg):
        starts.append(cur)
        cur = cur * take(gam_m, pos) + _bdot(cur, take(mp, pos)) + take(np_, pos)
    slot = {key: i for i, key in enumerate(lanes_of)}
    s_start = jnp.concatenate(
        [starts[c % cpg][slot[(c // cpg, s, p)]][None] for (s, p, c) in order], axis=0)
    finals = {(q * nseq + s, p): cur[i] for i, (q, s, p) in enumerate(lanes_of)}
    y = (_bdot_nt(seqsplit(rh), s_start) + seqsplit(y1))[:, :steps]
    for i, (s, p, c) in enumerate(order):
        r0 = c * CHUNK + s * steps
        if len(y_ref.shape) == 3:
            per = y_ref.shape[1]
            y_ref[r0 // per, r0 % per:r0 % per + steps, p * LANES:(p + 1) * LANES] = y[i]
        else:
            y_ref[r0:r0 + steps, p * LANES:(p + 1) * LANES] = y[i]
    for (s, p), h in finals.items():
        h_ref[s, p] = h
        hout_ref[n_prev, 0, s, p, 0] = h[:HEAD_DIM, :HEAD_DIM]
        hout_ref[n_prev, 0, s, p, 1] = h[HEAD_DIM:, HEAD_DIM:]


def _scan_call(sc, h0, prev, l, *, steps, n_chunks, grid, row_block, row_map):
    nst = h0.shape[2]
    li = min(l, h0.shape[0] - 1)
    n_prev = 0 if prev is None else prev.shape[0]
    st_block = (1, nst, PAIRS, 2, HEAD_DIM, HEAD_DIM)
    rb = lambda w: pl.BlockSpec(row_block + (w,), row_map)
    hb = pl.BlockSpec((None,) + st_block, lambda b, t: (li, b, 0, 0, 0, 0, 0))
    stacked = lambda n: pl.BlockSpec((n,) + st_block, lambda b, t: (0, b, 0, 0, 0, 0, 0))
    return pl.pallas_call(
        functools.partial(_scan_kernel, steps=steps, n_chunks=n_chunks, n_prev=n_prev),
        grid=grid, in_specs=[rb(SC_WIDTH), hb] + ([stacked(n_prev)] if n_prev else []),
        out_specs=[rb(D_A), stacked(n_prev + 1)],
        out_shape=[jax.ShapeDtypeStruct(sc.shape[:-1] + (D_A,), F32),
                   jax.ShapeDtypeStruct((n_prev + 1,) + h0.shape[1:], F32)],
        scratch_shapes=[pltpu.VMEM((nst, PAIRS, LANES, LANES), F32)],
        compiler_params=pltpu.CompilerParams(dimension_semantics=("arbitrary", "arbitrary"),
                                             vmem_limit_bytes=VMEM_LIMIT),
        name="scan",
    )(sc, h0, *([prev] if n_prev else []))


def _post_kernel(y_ref, r_ref, k_ref, v_ref, pg_ref, x_ref, ones_ref, rk_ref, lng_ref,
                 lnb_ref, waup_ref, wo_ref, n2_ref, w1_ref, w2_ref, fn_ref, o_ref, *, final):
    ones = ones_ref[...]
    y = y_ref[...]
    mu = _seg_sum(y, ones) * (1.0 / HEAD_DIM)
    d = y - mu
    var = _seg_sum(d * d, ones) * (1.0 / HEAD_DIM)
    yn = d * lax.rsqrt(var + GN_EPS) * lng_ref[...] + lnb_ref[...]
    bonus = _seg_sum(r_ref[...] * k_ref[...] * rk_ref[...], ones) * v_ref[...]
    a_out = _dot((yn + bonus) * pg_ref[:, PG_G:PG_GA], waup_ref[...])
    merged = pg_ref[:, PG_GA:PG_GBB] * a_out + pg_ref[:, PG_GBB:PG_WIDTH]
    x1 = x_ref[...] + _dot(merged, wo_ref[...])
    hb = _rms(x1, n2_ref[...]).astype(BF16)
    x2 = x1
    for c in range(0, D_FF, FF_CHUNK):
        f = jnp.maximum(jnp.dot(hb, w1_ref[:, c:c + FF_CHUNK], preferred_element_type=F32), 0.0)
        x2 = x2 + _dot(f * f, w2_ref[c:c + FF_CHUNK, :])
    o_ref[...] = _rms(x2, fn_ref[...]) if final else x2


def _post_call(y, sc, pg, x, wts, l, *, final, tm):
    n = y.shape[0]
    rb = lambda a: pl.BlockSpec((tm, a.shape[-1]), lambda i: (i, 0))
    section = lambda c0: pl.BlockSpec((tm, D_A), lambda i: (i, c0 // D_A))
    return pl.pallas_call(
        functools.partial(_post_kernel, final=final),
        grid=(n // tm,),
        in_specs=[rb(y), section(SC_R), section(SC_K), section(SC_V), rb(pg), rb(x)]
                 + [_layer_spec(w, l) for w in wts],
        out_specs=pl.BlockSpec((tm, D_MODEL), lambda i: (i, 0)),
        out_shape=jax.ShapeDtypeStruct((n, D_MODEL), F32),
        compiler_params=pltpu.CompilerParams(dimension_semantics=("arbitrary",),
                                             vmem_limit_bytes=VMEM_LIMIT),
        name="post",
    )(y, sc, sc, sc, pg, x, *wts)


def _stacked_weights(w_in, mu_shift, decay0, w_decay2, a0, w_a2, w_g2, k_k, k_a, r_k, ln_x_g, ln_x_b,
                     w_a_up, w_pool, pool_scale, w_b_up, w_o, norm1_g, norm2_g, w_ff1, w_ff2, final_norm_g):
    depth = w_in.shape[0]
    row = lambda a: a.reshape(a.shape[0], 1, -1)
    zpad = jnp.zeros((depth, LORA_WA // 2, D_A), F32)
    seg = jnp.arange(MXU_DIM) // HEAD_DIM
    ones_bd = (seg[:, None] == seg[None, :]).astype(BF16)[None]
    prep = [row(norm1_g), w_in.astype(BF16), row(mu_shift), row(decay0),
            jnp.concatenate([w_decay2, zpad], axis=1).astype(BF16), row(a0),
            jnp.concatenate([zpad, w_a2], axis=1).astype(BF16), w_g2.astype(BF16),
            row(k_k), row(k_a), ones_bd, w_pool.astype(BF16), row(pool_scale), w_b_up.astype(BF16)]
    post = [ones_bd, row(r_k), row(ln_x_g), row(ln_x_b), w_a_up.astype(BF16), w_o.astype(BF16),
            row(norm2_g), w_ff1.astype(BF16), w_ff2.astype(BF16), final_norm_g.reshape(1, 1, -1)]
    return prep, post


def kernel(x_prompt, x_sample, state_shift, state_pool, state_wkv, norm1_g, w_in, mu_shift, decay0,
           w_decay2, a0, w_a2, w_g2, k_k, k_a, r_k, ln_x_g, ln_x_b, w_a_up, w_pool, pool_scale, w_b_up,
           w_o, norm2_g, w_ff1, w_ff2, final_norm_g):
    depth = w_in.shape[0]
    bp, tp, _ = x_prompt.shape
    bs, ts, _ = x_sample.shape
    tt_p = 512
    nb_s = 32
    ts_p = 256
    nb_scan = 4
    tm_post = 512
    nc_s = 2
    nseq = CHUNK // ts
    hist = state_pool.shape[2]
    prep_w, post_w = _stacked_weights(
        w_in, mu_shift, decay0, w_decay2, a0, w_a2, w_g2, k_k, k_a, r_k, ln_x_g, ln_x_b, w_a_up, w_pool,
        pool_scale, w_b_up, w_o, norm1_g, norm2_g, w_ff1, w_ff2, final_norm_g)

    p_shift0 = jnp.zeros((bp, SUBLANES, D_SHIFT), F32)
    p_pool0 = jnp.zeros((bp, POOL_CARRY, D_POOL), F32)
    p_wkv0 = jnp.zeros((1, bp // nb_scan, nb_scan, PAIRS, 2, HEAD_DIM, HEAD_DIM), F32)
    s_pool0 = jnp.swapaxes(state_pool, 1, 2)
    s_wkv0 = state_wkv.reshape(depth, bs // (nseq * nc_s), nseq * nc_s, PAIRS, 2, HEAD_DIM, HEAD_DIM)

    xp = x_prompt
    xs = x_sample.reshape(bs * ts, D_MODEL)
    outs = {k: [] for k in ("p_shift", "s_shift")}
    p_pool = p_wkv = s_pool = s_wkv = None
    for l in range(depth):
        final = l == depth - 1

        res = _prep_call(
            xp, p_shift0, p_pool0, p_pool, prep_w, l, nb=1, tt=tt_p, pos0=0, grid=(bp, tp // tt_p),
            row_block=(1, tt_p), row_map=lambda b, t: (b, t, 0),
            shift_shape=p_shift0.shape, shift_in_map=lambda b, t: (b, 0, 0),
            shift_out_map=lambda b, t: (b, 0, 0), pool_shape=p_pool0.shape,
            pool_block=(1, POOL_CARRY, D_POOL),
            pool_in_spec=pl.BlockSpec((1, POOL_CARRY, D_POOL), lambda b, t: (b, 0, 0)),
            pool_stack_map=lambda b, t: (0, b, 0, 0))
        sc, pg, sh, p_pool = res
        outs["p_shift"].append(sh[:, SUBLANES - 1])
        y, p_wkv = _scan_call(
            sc, p_wkv0, p_wkv, l, steps=CHUNK, n_chunks=nb_scan * ts_p // CHUNK,
            grid=(bp // nb_scan, tp // ts_p), row_block=(nb_scan, ts_p), row_map=lambda b, t: (b, t, 0))
        flat = lambda z: z.reshape(bp * tp, z.shape[-1])
        xp = _post_call(flat(y), flat(sc), flat(pg), flat(xp), post_w, l, final=final,
                        tm=tm_post).reshape(bp, tp, D_MODEL)

        res = _prep_call(
            xs.reshape(bs, ts, D_MODEL), state_shift, s_pool0, s_pool, prep_w, l, nb=nb_s, tt=ts,
            pos0=PAST_LEN, grid=(bs // nb_s, 1), row_block=(nb_s, ts), row_map=lambda b, t: (b, 0, 0),
            shift_shape=(1, bs, D_SHIFT), shift_in_map=lambda b, t, l=l: (l, b, 0),
            shift_out_map=lambda b, t: (0, b, 0), pool_shape=(hist, bs, D_POOL),
            pool_block=(hist, nb_s, D_POOL),
            pool_in_spec=pl.BlockSpec((None, hist, nb_s, D_POOL), lambda b, t, l=l: (l, 0, b, 0)),
            pool_stack_map=lambda b, t: (0, 0, b, 0))
        sc, pg = (z.reshape(bs * ts, z.shape[-1]) for z in res[:2])
        outs["s_shift"].append(res[2][0])
        s_pool = res[3]
        y, s_wkv = _scan_call(
            sc, s_wkv0, s_wkv, l, steps=ts, n_chunks=nc_s,
            grid=(bs // (nseq * nc_s), 1), row_block=(nc_s * CHUNK,), row_map=lambda b, t: (b, 0))
        xs = _post_call(y, sc, pg, xs, post_w, l, final=final, tm=tm_post)

    st = lambda key: jnp.stack(outs[key])
    wkv = lambda h, n: h.reshape(depth, n, N_HEADS, HEAD_DIM, HEAD_DIM)
    return (xp, xs.reshape(bs, ts, D_MODEL), st("p_shift"), p_pool[:, :, 1:], wkv(p_wkv, bp),
            st("s_shift"), jnp.swapaxes(s_pool, 1, 2), wkv(s_wkv, bs))
```
